```python
import jax, jax.numpy as jnp
from jax import lax
import numpy as np

D_MODEL = 2048
BATCH = 2
SEQ = 4096
DEPTH = 2
DEC_BATCH = 8
DEC_SEQ = 8
PAST_LEN = 16384
PAGE_SIZE = 128

N_A_LAYERS = DEPTH // 2
N_B_LAYERS = DEPTH - N_A_LAYERS
N_HEADS = 16
HEAD_DIM = D_MODEL // N_HEADS
D_FF = (11 * D_MODEL) // 4
CONV_A_WIDTH = 31
CONV_F_WIDTH = 3
PLE_DIM = 256
Q_BLOCK = 128
EPS = 1e-6
NEG_INF = -1e30

kernel_name = 'conformer_fox_yoco_decoder_step'


def rmsnorm(x, g):
    xf = x.astype(jnp.float32)
    y = xf * lax.rsqrt(jnp.mean(xf * xf, axis=-1, keepdims=True) + EPS)
    return (y * g.astype(jnp.float32)).astype(x.dtype)


def layernorm(x, g, b):
    xf = x.astype(jnp.float32)
    mu = jnp.mean(xf, axis=-1, keepdims=True)
    xc = xf - mu
    y = xc * lax.rsqrt(jnp.mean(xc * xc, axis=-1, keepdims=True) + EPS)
    return (y * g.astype(jnp.float32) + b.astype(jnp.float32)).astype(x.dtype)


def causal_dwconv(u, prev, w, b):
    full = jnp.concatenate([prev.astype(u.dtype), u], axis=1)
    out = lax.conv_general_dilated(full, w[:, None, :].astype(u.dtype), window_strides=(1,), padding='VALID',
                                   dimension_numbers=('NWC', 'WIO', 'NWC'), feature_group_count=u.shape[-1])
    return out + b, full[:, -(w.shape[0] - 1):]


def conformer_conv(xn, prev, w_pw1, w_dw, b_dw, ln_g, ln_b, w_pw2):
    a = xn @ w_pw1
    u = a[..., :D_MODEL] * jax.nn.sigmoid(a[..., D_MODEL:])
    c, buf = causal_dwconv(u, prev, w_dw, b_dw)
    c = layernorm(c, ln_g, ln_b)
    return jax.nn.silu(c) @ w_pw2, buf


def conv_ffn(xn, prev, w_up, w_gate, w_dw, b_dw, w_down):
    u = xn @ w_up
    c, buf = causal_dwconv(u, prev, w_dw, b_dw)
    return (jax.nn.silu(c) * (xn @ w_gate)) @ w_down, buf


def shared_kv(h, g_kv, w_k, w_v, g_k, w_f, b_f):
    n, t, _ = h.shape
    xn = rmsnorm(h, g_kv)
    k = rmsnorm((xn @ w_k).reshape(n, t, N_HEADS, HEAD_DIM), g_k)
    v = (xn @ w_v).reshape(n, t, N_HEADS, HEAD_DIM)
    logf = jax.nn.log_sigmoid((xn @ w_f + b_f).astype(jnp.float32))
    return k, v, logf


def fox_block(qf, cq, qpos, kf, vf, ck, kpos):
    s = jnp.einsum('nqhd,nkhd->nhqk', qf, kf)
    s = s + jnp.transpose(cq, (0, 2, 1))[:, :, :, None] - jnp.transpose(ck, (0, 2, 1))[:, :, None, :]
    s = jnp.where((kpos[None, :] <= qpos[:, None])[None, None], s, NEG_INF)
    p = jax.nn.softmax(s, axis=-1)
    return jnp.einsum('nhqk,nkhd->nqhd', p, vf)


def fox_prompt(q, k, v, logf):
    n, t, h, dh = q.shape
    nb = t // Q_BLOCK
    c = jnp.cumsum(logf.astype(jnp.float32), axis=1)
    pos = jnp.arange(t)
    qf = q.astype(jnp.float32) * (HEAD_DIM ** -0.5)
    kf = k.astype(jnp.float32)
    vf = v.astype(jnp.float32)
    qb = jnp.transpose(qf.reshape(n, nb, Q_BLOCK, h, dh), (1, 0, 2, 3, 4))
    cb = jnp.transpose(c.reshape(n, nb, Q_BLOCK, h), (1, 0, 2, 3))
    pb = pos.reshape(nb, Q_BLOCK)
    out = lax.map(lambda a: fox_block(a[0], a[1], a[2], kf, vf, c, pos), (qb, cb, pb))
    return jnp.transpose(out, (1, 0, 2, 3, 4)).reshape(n, t, h * dh).astype(q.dtype)


def fox_sample(q, k_new, v_new, lf_new, cache_k, cache_v, cache_logf, page_table):
    n, t, h, dh = q.shape
    past = page_table.shape[1] * PAGE_SIZE
    kp = cache_k[page_table].reshape(n, past, h, dh).astype(jnp.float32)
    vp = cache_v[page_table].reshape(n, past, h, dh).astype(jnp.float32)
    c_past = jnp.cumsum(cache_logf[page_table].reshape(n, past, h).astype(jnp.float32), axis=1)
    c_new = c_past[:, -1:] + jnp.cumsum(lf_new.astype(jnp.float32), axis=1)
    qf = q.astype(jnp.float32) * (HEAD_DIM ** -0.5)
    cq = jnp.transpose(c_new, (0, 2, 1))[:, :, :, None]
    s_past = jnp.einsum('nqhd,nkhd->nhqk', qf, kp) + cq - jnp.transpose(c_past, (0, 2, 1))[:, :, None, :]
    s_new = jnp.einsum('nqhd,nkhd->nhqk', qf, k_new.astype(jnp.float32)) + cq - jnp.transpose(c_new, (0, 2, 1))[:, :, None, :]
    tri = jnp.arange(t)[None, :] <= jnp.arange(t)[:, None]
    s_new = jnp.where(tri[None, None], s_new, NEG_INF)
    p = jax.nn.softmax(jnp.concatenate([s_past, s_new], axis=-1), axis=-1)
    out = (jnp.einsum('nhqk,nkhd->nqhd', p[..., :past], vp)
           + jnp.einsum('nhqk,nkhd->nqhd', p[..., past:], v_new.astype(jnp.float32)))
    return out.reshape(n, t, h * dh).astype(q.dtype)


def trunk(x, p, a_prev, f_prev, attend,
          g_mix_a, w_pw1, w_dwa, b_dwa, ln_g, ln_b, w_pw2,
          g_kv, w_k, w_v, g_k, w_f, b_f,
          g_mix_b, w_q, g_q, w_o,
          g_ffn, w_up, w_gate, w_fdw, b_fdw, w_down,
          g_ple, w_ple_gate, w_ple_proj):
    n, t, _ = x.shape
    h = x
    a_bufs, f_bufs = [], []
    k = v = logf = None
    for i in range(DEPTH):
        if i < N_A_LAYERS:
            y, buf = conformer_conv(rmsnorm(h, g_mix_a[i]), a_prev[i], w_pw1[i], w_dwa[i], b_dwa[i],
                                    ln_g[i], ln_b[i], w_pw2[i])
            a_bufs.append(buf)
        else:
            if i == N_A_LAYERS:
                k, v, logf = shared_kv(h, g_kv, w_k, w_v, g_k, w_f, b_f)
            j = i - N_A_LAYERS
            q = rmsnorm((rmsnorm(h, g_mix_b[j]) @ w_q[j]).reshape(n, t, N_HEADS, HEAD_DIM), g_q[j])
            y = attend(q, k, v, logf) @ w_o[j]
        h = h + y
        y, buf = conv_ffn(rmsnorm(h, g_ffn[i]), f_prev[i], w_up[i], w_gate[i], w_fdw[i], b_fdw[i], w_down[i])
        f_bufs.append(buf)
        h = h + y
        h = h + jax.nn.sigmoid(rmsnorm(h, g_ple[i]) @ w_ple_gate[i]) * (p[i] @ w_ple_proj[i])
    return h, jnp.stack(a_bufs), jnp.stack(f_bufs), k, v, logf


def setup_inputs(seed: int = 0) -> dict:
    key = jax.random.key(seed)
    keys = jax.random.split(key, 48)
    ctr = [0]

    def nrm(shape, scale=1.0):
        ctr[0] += 1
        return jax.random.normal(keys[ctr[0] - 1], shape, jnp.float32) * scale

    def gain(shape):
        return 1.0 + nrm(shape, 0.05)

    d, f, h, dh = D_MODEL, D_FF, N_HEADS, HEAD_DIM
    na, nb = N_A_LAYERS, N_B_LAYERS
    n_pages = PAST_LEN // PAGE_SIZE
    n_used = DEC_BATCH * n_pages
    n_pool = n_used + max(1, n_used // 4)
    out = {}
    out['x_prompt'] = nrm((BATCH, SEQ, d))
    out['x_sample'] = nrm((DEC_BATCH, DEC_SEQ, d))
    out['state_conv_a'] = nrm((na, DEC_BATCH, CONV_A_WIDTH - 1, d), 0.5)
    out['state_ffn'] = nrm((DEPTH, DEC_BATCH, CONV_F_WIDTH - 1, f))
    out['cache_k'] = nrm((n_pool, PAGE_SIZE, h, dh))
    out['cache_v'] = nrm((n_pool, PAGE_SIZE, h, dh))
    out['cache_logf'] = jax.nn.log_sigmoid(3.0 + nrm((n_pool, PAGE_SIZE, h), 0.5))
    ctr[0] += 1
    out['page_table'] = jax.random.permutation(keys[ctr[0] - 1], n_pool)[:n_used].reshape(DEC_BATCH, n_pages).astype(jnp.int32)
    out['p_prompt'] = nrm((DEPTH, BATCH, SEQ, PLE_DIM))
    out['p_sample'] = nrm((DEPTH, DEC_BATCH, DEC_SEQ, PLE_DIM))
    out['g_mix_a'] = gain((na, d))
    out['w_pw1'] = nrm((na, d, 2 * d), d ** -0.5)
    out['w_dwa'] = nrm((na, CONV_A_WIDTH, d), CONV_A_WIDTH ** -0.5)
    out['b_dwa'] = nrm((na, d), 0.02)
    out['ln_g'] = gain((na, d))
    out['ln_b'] = nrm((na, d), 0.02)
    out['w_pw2'] = nrm((na, d, d), d ** -0.5)
    out['g_kv'] = gain((d,))
    out['w_k'] = nrm((d, d), d ** -0.5)
    out['w_v'] = nrm((d, d), d ** -0.5)
    out['g_k'] = gain((dh,))
    out['w_f'] = nrm((d, h), 0.5 * d ** -0.5)
    out['b_f'] = 3.0 + nrm((h,), 0.5)
    out['g_mix_b'] = gain((nb, d))
    out['w_q'] = nrm((nb, d, d), d ** -0.5)
    out['g_q'] = gain((nb, dh))
    out['w_o'] = nrm((nb, d, d), d ** -0.5)
    out['g_ffn'] = gain((DEPTH, d))
    out['w_up'] = nrm((DEPTH, d, f), d ** -0.5)
    out['w_gate'] = nrm((DEPTH, d, f), d ** -0.5)
    out['w_fdw'] = nrm((DEPTH, CONV_F_WIDTH, f), CONV_F_WIDTH ** -0.5)
    out['b_fdw'] = nrm((DEPTH, f), 0.02)
    out['w_down'] = nrm((DEPTH, f, d), f ** -0.5)
    out['g_ple'] = gain((DEPTH, d))
    out['w_ple_gate'] = nrm((DEPTH, d, d), d ** -0.5)
    out['w_ple_proj'] = nrm((DEPTH, PLE_DIM, d), PLE_DIM ** -0.5)
    return out


def reference(x_prompt, x_sample, state_conv_a, state_ffn, cache_k, cache_v, cache_logf, page_table,
              p_prompt, p_sample,
              g_mix_a, w_pw1, w_dwa, b_dwa, ln_g, ln_b, w_pw2,
              g_kv, w_k, w_v, g_k, w_f, b_f,
              g_mix_b, w_q, g_q, w_o,
              g_ffn, w_up, w_gate, w_fdw, b_fdw, w_down,
              g_ple, w_ple_gate, w_ple_proj):
    nb_p = x_prompt.shape[0]
    a_prev_p = [jnp.zeros((nb_p, CONV_A_WIDTH - 1, D_MODEL), x_prompt.dtype) for _ in range(N_A_LAYERS)]
    f_prev_p = [jnp.zeros((nb_p, CONV_F_WIDTH - 1, D_FF), x_prompt.dtype) for _ in range(DEPTH)]
    y_prompt, conv_a_p, ffn_p, k_p, v_p, lf_p = trunk(
        x_prompt, p_prompt, a_prev_p, f_prev_p, fox_prompt,
        g_mix_a, w_pw1, w_dwa, b_dwa, ln_g, ln_b, w_pw2,
        g_kv, w_k, w_v, g_k, w_f, b_f,
        g_mix_b, w_q, g_q, w_o,
        g_ffn, w_up, w_gate, w_fdw, b_fdw, w_down,
        g_ple, w_ple_gate, w_ple_proj)

    def attend_sample(q, k, v, logf):
        return fox_sample(q, k, v, logf, cache_k, cache_v, cache_logf, page_table)

    a_prev_s = [state_conv_a[i] for i in range(N_A_LAYERS)]
    f_prev_s = [state_ffn[i] for i in range(DEPTH)]
    y_sample, conv_a_s, ffn_s, k_s, v_s, lf_s = trunk(
        x_sample, p_sample, a_prev_s, f_prev_s, attend_sample,
        g_mix_a, w_pw1, w_dwa, b_dwa, ln_g, ln_b, w_pw2,
        g_kv, w_k, w_v, g_k, w_f, b_f,
        g_mix_b, w_q, g_q, w_o,
        g_ffn, w_up, w_gate, w_fdw, b_fdw, w_down,
        g_ple, w_ple_gate, w_ple_proj)
    return (y_prompt, y_sample, conv_a_p, ffn_p, k_p, v_p, lf_p, conv_a_s, ffn_s, k_s, v_s, lf_s)
```

```python
import functools

import jax
import jax.numpy as jnp
from jax import lax
from jax.experimental import pallas as pl
from jax.experimental.pallas import tpu as pltpu

F32 = jnp.float32
BF16 = jnp.bfloat16

HEAD_DIM = 128
CONV_A_WIDTH = 31
CONV_F_WIDTH = 3
PAGE_SIZE = 128
EPS = 1e-6
NEG_INF = -1e30

LANES = 128
SUBLANES = 8
HALO_ROWS = 32
VMEM_LIMIT = 56 * 1024 * 1024
PAGES_PER_STEP = 4


def _params(n_axes):
    return pltpu.CompilerParams(
        dimension_semantics=("arbitrary",) * n_axes,
        vmem_limit_bytes=VMEM_LIMIT)


def _dot(a, b):
    return jnp.dot(a, b, preferred_element_type=F32)


def _dot_nt(a, b):
    return lax.dot_general(a, b, (((1,), (1,)), ((), ())),
                           preferred_element_type=F32)


def _sigmoid(x):
    return jax.nn.sigmoid(x)


def _log_sigmoid(z):
    return -(jnp.maximum(-z, 0.0) + jnp.log1p(jnp.exp(-jnp.abs(z))))


def _rms_rows(x):
    ms = jnp.mean(x * x, axis=-1, keepdims=True)
    return x * lax.rsqrt(ms + EPS)


def _split3(x):
    hi = x.astype(BF16)
    r1 = x - hi.astype(F32)
    mid = r1.astype(BF16)
    lo = (r1 - mid.astype(F32)).astype(BF16)
    return hi, mid, lo


def _dot3(x, w_bf16):
    hi, mid, lo = _split3(x)
    return _dot(hi, w_bf16) + _dot(mid, w_bf16) + _dot(lo, w_bf16)


def _glu_kernel(x_ref, g_ref, w1_ref, w2_ref, u_ref, xn_ref):
    @pl.when(pl.program_id(1) == 0)
    def _():
        xn_ref[...] = (_rms_rows(x_ref[...]) * g_ref[...]).astype(BF16)

    xn = xn_ref[...]
    a1 = _dot(xn, w1_ref[...])
    a2 = _dot(xn, w2_ref[...])
    u_ref[...] = a1 * _sigmoid(a2)


def _glu_call(h, g, w_pw1, tm, tn):
    m, d = h.shape
    nj = d // tn
    return pl.pallas_call(
        _glu_kernel,
        grid=(m // tm, nj),
        in_specs=[
            pl.BlockSpec((tm, d), lambda i, j: (i, 0)),
            pl.BlockSpec((1, d), lambda i, j: (0, 0)),
            pl.BlockSpec((d, tn), lambda i, j: (0, j)),
            pl.BlockSpec((d, tn), lambda i, j: (0, j + nj)),
        ],
        out_specs=pl.BlockSpec((tm, tn), lambda i, j: (i, j)),
        out_shape=jax.ShapeDtypeStruct((m, d), F32),
        scratch_shapes=[pltpu.VMEM((tm, d), BF16)],
        compiler_params=_params(2),
        name="glu",
    )(h, g, w_pw1, w_pw1)


def _conva_kernel(u_ref, halo_ref, prev_ref, wdw_ref, bdw_ref, lng_ref, lnb_ref,
                  w2_ref, h_ref, o_ref, full_ref, c_ref, *, tm, tps, rb):
    i = pl.program_id(0)
    d = u_ref.shape[1]
    first = (i % tps) == 0
    full_ref[0:HALO_ROWS, :] = jnp.where(first, prev_ref[0], halo_ref[...])
    full_ref[HALO_ROWS:HALO_ROWS + tm, :] = u_ref[...]
    off = HALO_ROWS - (CONV_A_WIDTH - 1)
    for r0 in range(0, tm, rb):
        for c0 in range(0, d, LANES):
            acc = jnp.broadcast_to(bdw_ref[:, c0:c0 + LANES], (rb, LANES))
            for j in range(CONV_A_WIDTH):
                s = r0 + off + j
                acc = acc + (wdw_ref[j:j + 1, c0:c0 + LANES]
                             * full_ref[s:s + rb, c0:c0 + LANES])
            c_ref[r0:r0 + rb, c0:c0 + LANES] = acc
    c = c_ref[...]
    mu = jnp.mean(c, axis=-1, keepdims=True)
    xc = c - mu
    var = jnp.mean(xc * xc, axis=-1, keepdims=True)
    y = xc * lax.rsqrt(var + EPS) * lng_ref[...] + lnb_ref[...]
    a = (y * _sigmoid(y)).astype(BF16)
    o_ref[...] = h_ref[...] + _dot(a, w2_ref[...])


def _conva_call(u, prev32, w_dw, b_dw, ln_g, ln_b, w_pw2, h, tm, seq_len):
    m, d = u.shape
    tps = max(seq_len // tm, 1)
    rb = min(tm, 128)
    if tm % HALO_ROWS == 0:
        k = tm // HALO_ROWS
        halo_map = lambda i: (jnp.maximum(i * k - 1, 0), 0)
    else:
        assert tps == 1
        halo_map = lambda i: (0, 0)
    kern = functools.partial(_conva_kernel, tm=tm, tps=tps, rb=rb)
    return pl.pallas_call(
        kern,
        grid=(m // tm,),
        in_specs=[
            pl.BlockSpec((tm, d), lambda i: (i, 0)),
            pl.BlockSpec((HALO_ROWS, d), halo_map),
            pl.BlockSpec((1, HALO_ROWS, d), lambda i: (i // tps, 0, 0)),
            pl.BlockSpec((CONV_A_WIDTH, d), lambda i: (0, 0)),
            pl.BlockSpec((1, d), lambda i: (0, 0)),
            pl.BlockSpec((1, d), lambda i: (0, 0)),
            pl.BlockSpec((1, d), lambda i: (0, 0)),
            pl.BlockSpec((d, d), lambda i: (0, 0)),
            pl.BlockSpec((tm, d), lambda i: (i, 0)),
        ],
        out_specs=pl.BlockSpec((tm, d), lambda i: (i, 0)),
        out_shape=jax.ShapeDtypeStruct((m, d), F32),
        scratch_shapes=[pltpu.VMEM((HALO_ROWS + tm, d), F32),
                        pltpu.VMEM((tm, d), F32)],
        compiler_params=_params(1),
        name="conv_a",
    )(u, u, prev32, w_dw, b_dw, ln_g, ln_b, w_pw2, h)


def _ffn_body(i, j, h_ref, g_ref, wu_ref, wg_ref, wdw_ref, bdw_ref, wd_ref,
              o_ref, xn_ref, ubuf_ref, tm, set_prev_rows, fix_shifted):
    @pl.when(j == 0)
    def _():
        xn_ref[...] = (_rms_rows(h_ref[...]) * g_ref[...]).astype(BF16)

    xn = xn_ref[...]
    u = _dot(xn, wu_ref[...])
    gt = _dot(xn, wg_ref[...])
    ubuf_ref[SUBLANES:SUBLANES + tm, :] = u
    set_prev_rows()
    u1 = ubuf_ref[SUBLANES - 1:SUBLANES - 1 + tm, :]
    u2 = ubuf_ref[SUBLANES - 2:SUBLANES - 2 + tm, :]
    u1, u2 = fix_shifted(u1, u2)
    c = (wdw_ref[0:1, :] * u2 + wdw_ref[1:2, :] * u1 + wdw_ref[2:3, :] * u
         + bdw_ref[...])
    act = (c * _sigmoid(c) * gt).astype(BF16)
    part = _dot(act, wd_ref[...])

    @pl.when(j == 0)
    def _():
        o_ref[...] = h_ref[...] + part

    @pl.when(j > 0)
    def _():
        o_ref[...] += part


def _ffn_seq_kernel(h_ref, g_ref, wu_ref, wg_ref, wdw_ref, bdw_ref, wd_ref,
                    prev_ref, o_ref, fb_ref, xn_ref, ubuf_ref, carry_ref,
                    *, tm, tps):
    i = pl.program_id(0)
    j = pl.program_id(1)
    first = (i % tps) == 0

    def set_prev_rows():
        @pl.when(first)
        def _():
            ubuf_ref[SUBLANES - 2:SUBLANES, :] = prev_ref[0]

        @pl.when(jnp.logical_not(first))
        def _():
            ubuf_ref[SUBLANES - 2:SUBLANES, :] = carry_ref[j]

    _ffn_body(i, j, h_ref, g_ref, wu_ref, wg_ref, wdw_ref, bdw_ref, wd_ref,
              o_ref, xn_ref, ubuf_ref, tm, set_prev_rows, lambda a, b: (a, b))
    tail = ubuf_ref[SUBLANES + tm - 2:SUBLANES + tm, :]
    carry_ref[j] = tail
    fb_ref[0] = tail


def _ffn_multi_kernel(h_ref, g_ref, wu_ref, wg_ref, wdw_ref, bdw_ref, wd_ref,
                      p1_ref, p2_ref, o_ref, u_ref, xn_ref, ubuf_ref,
                      *, tm, seq_len):
    i = pl.program_id(0)
    j = pl.program_id(1)

    def set_prev_rows():
        ubuf_ref[0:SUBLANES, :] = jnp.zeros((SUBLANES, ubuf_ref.shape[1]), F32)

    def fix_shifted(u1, u2):
        t = lax.broadcasted_iota(jnp.int32, (tm, 1), 0) % seq_len
        return (jnp.where(t >= 1, u1, p1_ref[...]),
                jnp.where(t >= 2, u2, p2_ref[...]))

    _ffn_body(i, j, h_ref, g_ref, wu_ref, wg_ref, wdw_ref, bdw_ref, wd_ref,
              o_ref, xn_ref, ubuf_ref, tm, set_prev_rows, fix_shifted)
    u_ref[...] = ubuf_ref[SUBLANES:SUBLANES + tm, :]


def _ffn_common_specs(tm, tn, d, f):
    return [
        pl.BlockSpec((tm, d), lambda i, j: (i, 0)),
        pl.BlockSpec((1, d), lambda i, j: (0, 0)),
        pl.BlockSpec((d, tn), lambda i, j: (0, j)),
        pl.BlockSpec((d, tn), lambda i, j: (0, j)),
        pl.BlockSpec((CONV_F_WIDTH, tn), lambda i, j: (0, j)),
        pl.BlockSpec((1, tn), lambda i, j: (0, j)),
        pl.BlockSpec((tn, d), lambda i, j: (j, 0)),
    ]


def _ffn_seq_call(h, g, w_up, w_gate, w_dw, b_dw, w_down, prev, tm, tn, seq_len):
    m, d = h.shape
    f = w_up.shape[1]
    n_seq = m // seq_len
    tps = seq_len // tm
    kern = functools.partial(_ffn_seq_kernel, tm=tm, tps=tps)
    return pl.pallas_call(
        kern,
        grid=(m // tm, f // tn),
        in_specs=_ffn_common_specs(tm, tn, d, f) + [
            pl.BlockSpec((1, CONV_F_WIDTH - 1, tn), lambda i, j: (i // tps, 0, j)),
        ],
        out_specs=[
            pl.BlockSpec((tm, d), lambda i, j: (i, 0)),
            pl.BlockSpec((1, CONV_F_WIDTH - 1, tn), lambda i, j: (i // tps, 0, j)),
        ],
        out_shape=[jax.ShapeDtypeStruct((m, d), F32),
                   jax.ShapeDtypeStruct((n_seq, CONV_F_WIDTH - 1, f), F32)],
        scratch_shapes=[pltpu.VMEM((tm, d), BF16),
                        pltpu.VMEM((SUBLANES + tm, tn), F32),
                        pltpu.VMEM((f // tn, CONV_F_WIDTH - 1, tn), F32)],
        compiler_params=_params(2),
        name="ffn_seq",
    )(h, g, w_up, w_gate, w_dw, b_dw, w_down, prev)


def _ffn_multi_call(h, g, w_up, w_gate, w_dw, b_dw, w_down, p1, p2, tn, seq_len):
    m, d = h.shape
    f = w_up.shape[1]
    kern = functools.partial(_ffn_multi_kernel, tm=m, seq_len=seq_len)
    return pl.pallas_call(
        kern,
        grid=(1, f // tn),
        in_specs=_ffn_common_specs(m, tn, d, f) + [
            pl.BlockSpec((m, tn), lambda i, j: (0, j)),
            pl.BlockSpec((m, tn), lambda i, j: (0, j)),
        ],
        out_specs=[
            pl.BlockSpec((m, d), lambda i, j: (0, 0)),
            pl.BlockSpec((m, tn), lambda i, j: (0, j)),
        ],
        out_shape=[jax.ShapeDtypeStruct((m, d), F32),
                   jax.ShapeDtypeStruct((m, f), F32)],
        scratch_shapes=[pltpu.VMEM((m, d), BF16),
                        pltpu.VMEM((SUBLANES + m, tn), F32)],
        compiler_params=_params(2),
        name="ffn_multi",
    )(h, g, w_up, w_gate, w_dw, b_dw, w_down, p1, p2)


def _ple_kernel(h_ref, hj_ref, g_ref, p_ref, wg_ref, wp_ref, o_ref, xn_ref, pb_ref):
    @pl.when(pl.program_id(1) == 0)
    def _():
        xn_ref[...] = (_rms_rows(h_ref[...]) * g_ref[...]).astype(BF16)
        pb_ref[...] = p_ref[...].astype(BF16)

    gate = _dot(xn_ref[...], wg_ref[...])
    proj = _dot(pb_ref[...], wp_ref[...])
    o_ref[...] = hj_ref[...] + _sigmoid(gate) * proj


def _ple_call(h, g, p, w_gate, w_proj, tm, tn):
    m, d = h.shape
    pd = p.shape[1]
    return pl.pallas_call(
        _ple_kernel,
        grid=(m // tm, d // tn),
        in_specs=[
            pl.BlockSpec((tm, d), lambda i, j: (i, 0)),
            pl.BlockSpec((tm, tn), lambda i, j: (i, j)),
            pl.BlockSpec((1, d), lambda i, j: (0, 0)),
            pl.BlockSpec((tm, pd), lambda i, j: (i, 0)),
            pl.BlockSpec((d, tn), lambda i, j: (0, j)),
            pl.BlockSpec((pd, tn), lambda i, j: (0, j)),
        ],
        out_specs=pl.BlockSpec((tm, tn), lambda i, j: (i, j)),
        out_shape=jax.ShapeDtypeStruct((m, d), F32),
        scratch_shapes=[pltpu.VMEM((tm, d), BF16), pltpu.VMEM((tm, pd), BF16)],
        compiler_params=_params(2),
        name="ple",
    )(h, h, g, p, w_gate, w_proj)


def _qkv_kernel(h_ref, gkv_ref, gmix_ref, wk_ref, wv_ref, wq_ref, gk_ref, gq_ref,
                wf_ref, bf_ref, wft_ref, bft_ref, tri_ref,
                k_ref, v_ref, kb_ref, vb_ref, q_ref, lf_ref, ct_ref,
                xkv_ref, xq_ref, carry_ref, *, tm, tn, tps):
    i = pl.program_id(0)
    j = pl.program_id(1)

    @pl.when(j == 0)
    def _():
        r = _rms_rows(h_ref[...])
        xkv = (r * gkv_ref[...]).astype(BF16)
        xkv_ref[...] = xkv
        xq_ref[...] = (r * gmix_ref[...]).astype(BF16)
        lf_ref[...] = _log_sigmoid(_dot(xkv, wf_ref[...]) + bf_ref[...])
        lft = _log_sigmoid(_dot_nt(wft_ref[...], xkv) + bft_ref[...])

        @pl.when((i % tps) == 0)
        def _():
            carry_ref[...] = jnp.zeros(carry_ref.shape, F32)

        ct = _dot3(lft, tri_ref[...]) + carry_ref[...]
        ct_ref[0] = ct
        carry_ref[...] = ct[:, tm - 1:tm]

    xkv = xkv_ref[...]
    kk = _dot(xkv, wk_ref[...])
    vv = _dot(xkv, wv_ref[...])
    qq = _dot(xq_ref[...], wq_ref[...])
    v_ref[...] = vv
    vb_ref[...] = vv.astype(BF16)
    scale = HEAD_DIM ** -0.5
    for c0 in range(0, tn, HEAD_DIM):
        sl = slice(c0, c0 + HEAD_DIM)
        kn = _rms_rows(kk[:, sl]) * gk_ref[...]
        k_ref[:, sl] = kn
        kb_ref[:, sl] = kn.astype(BF16)
        qn = _rms_rows(qq[:, sl]) * gq_ref[...]
        q_ref[:, sl] = (qn * scale).astype(BF16)


def _qkv_call(h, g_kv, g_mix, w_k, w_v, w_q, g_k, g_q, w_f_pad, b_f_pad,
              w_ft, b_ft, tri, tm, tn, seq_len):
    m, d = h.shape
    nh = w_ft.shape[0]
    tps = max(seq_len // tm, 1)
    span = tps * tm
    kern = functools.partial(_qkv_kernel, tm=tm, tn=tn, tps=tps)
    row = lambda i, j: (i, 0)
    const = lambda i, j: (0, 0)
    col = lambda i, j: (0, j)
    tile = lambda i, j: (i, j)
    return pl.pallas_call(
        kern,
        grid=(m // tm, d // tn),
        in_specs=[
            pl.BlockSpec((tm, d), row),
            pl.BlockSpec((1, d), const),
            pl.BlockSpec((1, d), const),
            pl.BlockSpec((d, tn), col),
            pl.BlockSpec((d, tn), col),
            pl.BlockSpec((d, tn), col),
            pl.BlockSpec((1, HEAD_DIM), const),
            pl.BlockSpec((1, HEAD_DIM), const),
            pl.BlockSpec((d, LANES), const),
            pl.BlockSpec((1, LANES), const),
            pl.BlockSpec((nh, d), const),
            pl.BlockSpec((nh, 1), const),
            pl.BlockSpec((tm, tm), const),
        ],
        out_specs=[
            pl.BlockSpec((tm, tn), tile),
            pl.BlockSpec((tm, tn), tile),
            pl.BlockSpec((tm, tn), tile),
            pl.BlockSpec((tm, tn), tile),
            pl.BlockSpec((tm, tn), tile),
            pl.BlockSpec((tm, LANES), row),
            pl.BlockSpec((1, nh, tm), lambda i, j: (i // tps, 0, i % tps)),
        ],
        out_shape=[
            jax.ShapeDtypeStruct((m, d), F32),
            jax.ShapeDtypeStruct((m, d), F32),
            jax.ShapeDtypeStruct((m, d), BF16),
            jax.ShapeDtypeStruct((m, d), BF16),
            jax.ShapeDtypeStruct((m, d), BF16),
            jax.ShapeDtypeStruct((m, LANES), F32),
            jax.ShapeDtypeStruct((m // span, nh, span), F32),
        ],
        scratch_shapes=[pltpu.VMEM((tm, d), BF16), pltpu.VMEM((tm, d), BF16),
                        pltpu.VMEM((nh, 1), F32)],
        compiler_params=_params(2),
        name="qkv",
    )(h, g_kv, g_mix, w_k, w_v, w_q, g_k, g_q, w_f_pad, b_f_pad, w_ft, b_ft, tri)


def _attn_kernel(q_ref, k_ref, v_ref, c_ref, o_ref, m_ref, l_ref, acc_ref, *, tq):
    qi = pl.program_id(2)
    q = q_ref[0]
    m_ref[...] = jnp.full(m_ref.shape, NEG_INF, F32)
    l_ref[...] = jnp.zeros(l_ref.shape, F32)
    acc_ref[...] = jnp.zeros(acc_ref.shape, F32)

    def block(kb, diagonal):
        start = pl.multiple_of(kb * tq, tq)
        k = k_ref[0, pl.ds(start, tq), :]
        v = v_ref[0, pl.ds(start, tq), :]
        s = _dot_nt(q, k) - c_ref[0, kb]
        if diagonal:
            r = lax.broadcasted_iota(jnp.int32, (tq, tq), 0)
            c = lax.broadcasted_iota(jnp.int32, (tq, tq), 1)
            s = jnp.where(c <= r, s, NEG_INF)
        m_old = m_ref[...]
        m_new = jnp.maximum(m_old, jnp.max(s, axis=-1, keepdims=True))
        alpha = jnp.exp(m_old - m_new)
        p = jnp.exp(s - m_new)
        l_ref[...] = alpha * l_ref[...] + jnp.sum(p, axis=-1, keepdims=True)
        acc_ref[...] = alpha * acc_ref[...] + _dot(p.astype(BF16), v)
        m_ref[...] = m_new

    def body(kb, carry):
        block(kb, False)
        return carry

    lax.fori_loop(0, qi, body, 0)
    block(qi, True)
    o_ref[0] = (acc_ref[...] / l_ref[...]).astype(o_ref.dtype)


def _attn_call(q, k, v, ct, n_seq, seq_len, tq):
    d = q.shape[-1]
    nh = d // HEAD_DIM
    nb = seq_len // tq
    kern = functools.partial(_attn_kernel, tq=tq)
    return pl.pallas_call(
        kern,
        grid=(n_seq, nh, nb),
        in_specs=[
            pl.BlockSpec((1, tq, HEAD_DIM), lambda n, h, i: (n, i, h)),
            pl.BlockSpec((1, seq_len, HEAD_DIM), lambda n, h, i: (n, 0, h)),
            pl.BlockSpec((1, seq_len, HEAD_DIM), lambda n, h, i: (n, 0, h)),
            pl.BlockSpec((1, nb, 1, tq), lambda n, h, i: (n * nh + h, 0, 0, 0)),
        ],
        out_specs=pl.BlockSpec((1, tq, HEAD_DIM), lambda n, h, i: (n, i, h)),
        out_shape=jax.ShapeDtypeStruct((n_seq, seq_len, d), BF16),
        scratch_shapes=[pltpu.VMEM((tq, 1), F32), pltpu.VMEM((tq, 1), F32),
                        pltpu.VMEM((tq, HEAD_DIM), F32)],
        compiler_params=_params(3),
        name="fox_prompt",
    )(q, k, v, ct)


def _sattn_kernel(pt_ref, qx_ref, *refs, n_groups):
    g_pages = PAGES_PER_STEP
    k_refs = refs[0:g_pages]
    v_refs = refs[g_pages:2 * g_pages]
    lf_refs = refs[2 * g_pages:3 * g_pages]
    (sfx_ref, kn_ref, vn_ref, cn_ref, sel_ref, o_ref,
     m_ref, l_ref, acc_ref, carry_ref) = refs[3 * g_pages:]
    g = pl.program_id(1)
    rows = qx_ref.shape[1]
    d = qx_ref.shape[2]
    nh = d // HEAD_DIM
    reps = rows // nh

    @pl.when(g == 0)
    def _():
        m_ref[...] = jnp.full(m_ref.shape, NEG_INF, F32)
        l_ref[...] = jnp.zeros(l_ref.shape, F32)
        acc_ref[...] = jnp.zeros(acc_ref.shape, F32)
        carry_ref[...] = jnp.zeros(carry_ref.shape, F32)

    qx = qx_ref[0]

    def update(s, pv_fn):
        m_old = m_ref[...]
        m_new = jnp.maximum(m_old, jnp.max(s, axis=-1, keepdims=True))
        alpha = jnp.exp(m_old - m_new)
        p = jnp.exp(s - m_new)
        l_ref[...] = alpha * l_ref[...] + jnp.sum(p, axis=-1, keepdims=True)
        acc_ref[...] = alpha * acc_ref[...] + pv_fn(p.astype(BF16))
        m_ref[...] = m_new

    lft = jnp.concatenate([r[0] for r in lf_refs], axis=1)
    suf = _dot3(lft, sfx_ref[...]) + carry_ref[...]
    carry_ref[...] = suf[:, 0:1] + lft[:, 0:1]
    bias = jnp.concatenate([suf] * reps, axis=0)
    s = jnp.concatenate([_dot_nt(qx, r[0].astype(BF16)) for r in k_refs], axis=1)

    def pv_past(p):
        out = _dot(p[:, 0:PAGE_SIZE], v_refs[0][0].astype(BF16))
        for e in range(1, g_pages):
            out = out + _dot(p[:, e * PAGE_SIZE:(e + 1) * PAGE_SIZE],
                             v_refs[e][0].astype(BF16))
        return out

    update(s + bias, pv_past)

    @pl.when(g == n_groups - 1)
    def _():
        sn = _dot_nt(qx, kn_ref[0]) - jnp.concatenate([cn_ref[0]] * reps, axis=0)
        r = lax.broadcasted_iota(jnp.int32, sn.shape, 0) // nh
        c = lax.broadcasted_iota(jnp.int32, sn.shape, 1)
        update(jnp.where(c <= r, sn, NEG_INF), lambda p: _dot(p, vn_ref[0]))
        out = acc_ref[...] / l_ref[...]
        rr = lax.broadcasted_iota(jnp.int32, out.shape, 0) % nh
        cc = lax.broadcasted_iota(jnp.int32, out.shape, 1) // HEAD_DIM
        own = jnp.where(rr == cc, out, 0.0).astype(BF16)
        o_ref[0] = _dot(sel_ref[...], own).astype(o_ref.dtype)


def _sattn_call(page_table, qx, cache_k, cache_v, lf_pool_t, sfx, k_new, v_new,
                cn_t, sel):
    b, rows, d = qx.shape
    n_pages = page_table.shape[1]
    n_groups = n_pages // PAGES_PER_STEP
    nh = d // HEAD_DIM
    t_new = sel.shape[0]

    def page_map(e):
        def f(bi, g, pt):
            return (pt[bi, (n_groups - 1 - g) * PAGES_PER_STEP + e], 0, 0)
        return f

    per_seq = lambda bi, g, pt: (bi, 0, 0)
    const2 = lambda bi, g, pt: (0, 0)
    in_specs = [pl.BlockSpec((1, rows, d), per_seq)]
    in_specs += [pl.BlockSpec((1, PAGE_SIZE, d), page_map(e)) for e in range(PAGES_PER_STEP)]
    in_specs += [pl.BlockSpec((1, PAGE_SIZE, d), page_map(e)) for e in range(PAGES_PER_STEP)]
    in_specs += [pl.BlockSpec((1, nh, PAGE_SIZE), page_map(e)) for e in range(PAGES_PER_STEP)]
    in_specs += [
        pl.BlockSpec(sfx.shape, const2),
        pl.BlockSpec((1, PAGE_SIZE, d), per_seq),
        pl.BlockSpec((1, PAGE_SIZE, d), per_seq),
        pl.BlockSpec((1, nh, PAGE_SIZE), per_seq),
        pl.BlockSpec(sel.shape, const2),
    ]
    grid_spec = pltpu.PrefetchScalarGridSpec(
        num_scalar_prefetch=1,
        grid=(b, n_groups),
        in_specs=in_specs,
        out_specs=pl.BlockSpec((1, t_new, d), per_seq),
        scratch_shapes=[pltpu.VMEM((rows, 1), F32), pltpu.VMEM((rows, 1), F32),
                        pltpu.VMEM((rows, d), F32), pltpu.VMEM((nh, 1), F32)],
    )
    kern = functools.partial(_sattn_kernel, n_groups=n_groups)
    return pl.pallas_call(
        kern,
        grid_spec=grid_spec,
        out_shape=jax.ShapeDtypeStruct((b, t_new, d), BF16),
        compiler_params=_params(2),
        name="fox_decode",
    )(page_table, qx, *([cache_k] * PAGES_PER_STEP), *([cache_v] * PAGES_PER_STEP),
      *([lf_pool_t] * PAGES_PER_STEP), sfx, k_new, v_new, cn_t, sel)


def _oproj_kernel(a_ref, w_ref, h_ref, o_ref):
    o_ref[...] = h_ref[...] + _dot(a_ref[...], w_ref[...])


def _oproj_call(a, w_o, h, tm, tn):
    m, d = h.shape
    return pl.pallas_call(
        _oproj_kernel,
        grid=(m // tm, d // tn),
        in_specs=[
            pl.BlockSpec((tm, d), lambda i, j: (i, 0)),
            pl.BlockSpec((d, tn), lambda i, j: (0, j)),
            pl.BlockSpec((tm, tn), lambda i, j: (i, j)),
        ],
        out_specs=pl.BlockSpec((tm, tn), lambda i, j: (i, j)),
        out_shape=jax.ShapeDtypeStruct((m, d), F32),
        compiler_params=_params(2),
        name="o_proj",
    )(a, w_o, h)


def _cumsum_matrix(n, group):
    r = jnp.arange(n)[:, None]
    c = jnp.arange(n)[None, :]
    return ((r <= c) & (r // group == c // group)).astype(BF16)


def kernel(x_prompt, x_sample, state_conv_a, state_ffn, cache_k, cache_v, cache_logf, page_table, p_prompt, p_sample, g_mix_a, w_pw1, w_dwa, b_dwa, ln_g, ln_b, w_pw2, g_kv, w_k, w_v, g_k, w_f, b_f, g_mix_b, w_q, g_q, w_o, g_ffn, w_up, w_gate, w_fdw, b_fdw, w_down, g_ple, w_ple_gate, w_ple_proj):
    n_p, t_p, d = x_prompt.shape
    n_s, t_s, _ = x_sample.shape
    depth = g_ffn.shape[0]
    n_a = g_mix_a.shape[0]
    f = w_up.shape[2]
    nh = d // HEAD_DIM
    m_s = n_s * t_s

    bf = lambda w: w.astype(BF16)
    w_pw1_b, w_pw2_b = bf(w_pw1), bf(w_pw2)
    w_k_b, w_v_b, w_q_b, w_o_b = bf(w_k), bf(w_v), bf(w_q), bf(w_o)
    w_up_b, w_gate_b, w_down_b = bf(w_up), bf(w_gate), bf(w_down)
    w_pg_b, w_pp_b = bf(w_ple_gate), bf(w_ple_proj)
    w_f_pad = jnp.pad(bf(w_f), ((0, 0), (0, LANES - nh)))
    b_f_pad = jnp.pad(b_f, (0, LANES - nh)).reshape(1, LANES)
    w_ft = bf(w_f).T
    b_ft = b_f.reshape(nh, 1)
    row = lambda a: a.reshape(1, -1)

    def trunk(x, p, n_seq, seq_len, a_prev32, tm, tn, tm_conv, ffn, attend, tri):
        h = x.reshape(n_seq * seq_len, d)
        a_us, f_bufs = [], []
        k = v = lf = None
        for i in range(depth):
            if i < n_a:
                u = _glu_call(h, row(g_mix_a[i]), w_pw1_b[i], tm, tn)
                h = _conva_call(u, a_prev32[i], w_dwa[i], row(b_dwa[i]), row(ln_g[i]),
                                row(ln_b[i]), w_pw2_b[i], h, tm_conv, seq_len)
                a_us.append(u.reshape(n_seq, seq_len, d))
            else:
                jb = i - n_a
                k, v, kb, vb, q, lf, ct = _qkv_call(
                    h, row(g_kv), row(g_mix_b[jb]), w_k_b, w_v_b, w_q_b[jb],
                    row(g_k), row(g_q[jb]), w_f_pad, b_f_pad, w_ft, b_ft, tri,
                    tm, tn, seq_len)
                a = attend(q, kb, vb, ct)
                h = _oproj_call(a, w_o_b[jb], h, tm, tn)
            h, fb = ffn(i, h)
            f_bufs.append(fb)
            h = _ple_call(h, row(g_ple[i]), p[i].reshape(n_seq * seq_len, -1),
                          w_pg_b[i], w_pp_b[i], tm, tn)
        shp = (n_seq, seq_len, nh, HEAD_DIM)
        return (h.reshape(n_seq, seq_len, d), a_us, jnp.stack(f_bufs),
                k.reshape(shp), v.reshape(shp),
                lf[:, :nh].reshape(n_seq, seq_len, nh))

    tm_p, tn_p, tq = 512, 512, 512

    def ffn_prompt(i, h):
        prev = jnp.zeros((n_p, CONV_F_WIDTH - 1, f), F32)
        return _ffn_seq_call(h, row(g_ffn[i]), w_up_b[i], w_gate_b[i], w_fdw[i],
                             row(b_fdw[i]), w_down_b[i], prev, tm_p, tn_p, t_p)

    def attend_prompt(q, kb, vb, ct):
        shp = (n_p, t_p, d)
        ct4 = ct.reshape(n_p * nh, t_p // tq, 1, tq)
        a = _attn_call(q.reshape(shp), kb.reshape(shp), vb.reshape(shp), ct4,
                       n_p, t_p, tq)
        return a.reshape(n_p * t_p, d)

    a_prev_p = [jnp.zeros((n_p, HALO_ROWS, d), F32) for _ in range(n_a)]
    (y_p, a_us_p, ffn_p, k_p, v_p, lf_p) = trunk(
        x_prompt, p_prompt, n_p, t_p, a_prev_p, tm_p, tn_p, 256, ffn_prompt,
        attend_prompt, _cumsum_matrix(tm_p, tm_p))
    conv_a_p = jnp.stack([u[:, t_p - (CONV_A_WIDTH - 1):] for u in a_us_p])

    def ffn_sample(i, h):
        st = state_ffn[i]
        z = jnp.zeros((n_s, t_s, f), F32)
        p1 = z.at[:, 0].set(st[:, 1]).reshape(m_s, f)
        p2 = z.at[:, 0].set(st[:, 0]).at[:, 1].set(st[:, 1]).reshape(m_s, f)
        h, u = _ffn_multi_call(h, row(g_ffn[i]), w_up_b[i], w_gate_b[i], w_fdw[i],
                               row(b_fdw[i]), w_down_b[i], p1, p2, tn_p, t_s)
        full = jnp.concatenate([st, u.reshape(n_s, t_s, f)], axis=1)
        return h, full[:, -(CONV_F_WIDTH - 1):]

    n_pool = cache_k.shape[0]
    cache_k2 = cache_k.reshape(n_pool, PAGE_SIZE, d)
    cache_v2 = cache_v.reshape(n_pool, PAGE_SIZE, d)
    lf_pool_t = jnp.transpose(cache_logf, (0, 2, 1))
    gp = PAGES_PER_STEP * PAGE_SIZE
    rr = jnp.arange(gp)
    sfx = (rr[:, None] > rr[None, :]).astype(BF16)
    sel = (jnp.arange(t_s)[:, None] == (jnp.arange(t_s * nh)[None, :] // nh)).astype(BF16)
    eye = jnp.eye(nh, dtype=BF16)

    def attend_sample(q, kb, vb, ct):
        q5 = q.reshape(n_s, t_s, 1, nh, HEAD_DIM) * eye[None, None, :, :, None]
        qx = q5.reshape(n_s, t_s * nh, d)
        pad = ((0, 0), (0, PAGE_SIZE - t_s), (0, 0))
        k_new = jnp.pad(kb.reshape(n_s, t_s, d), pad)
        v_new = jnp.pad(vb.reshape(n_s, t_s, d), pad)
        cn = jnp.transpose(ct.reshape(nh, n_s, t_s), (1, 0, 2))
        cn_t = jnp.pad(cn, ((0, 0), (0, 0), (0, PAGE_SIZE - t_s)))
        a = _sattn_call(page_table, qx, cache_k2, cache_v2, lf_pool_t, sfx,
                        k_new, v_new, cn_t, sel)
        return a.reshape(m_s, d)

    a_prev_s = [jnp.pad(state_conv_a[i], ((0, 0), (HALO_ROWS - (CONV_A_WIDTH - 1), 0), (0, 0)))
                for i in range(n_a)]
    (y_s, a_us_s, ffn_s, k_s, v_s, lf_s) = trunk(
        x_sample, p_sample, n_s, t_s, a_prev_s, m_s, tn_p, t_s, ffn_sample,
        attend_sample, _cumsum_matrix(m_s, t_s))
    conv_a_s = jnp.stack([
        jnp.concatenate([state_conv_a[i], a_us_s[i]], axis=1)[:, -(CONV_A_WIDTH - 1):]
        for i in range(n_a)])

    return (y_p, y_s, conv_a_p, ffn_p, k_p, v_p, lf_p,
            conv_a_s, ffn_s, k_s, v_s, lf_s)
```

```python
import functools

import jax
import jax.numpy as jnp
from jax import lax
from jax.experimental import pallas as pl
from jax.experimental.pallas import tpu as pltpu

F32 = jnp.float32
BF16 = jnp.bfloat16

HEAD_DIM = 128
CONV_A_WIDTH = 31
CONV_F_WIDTH = 3
PAGE_SIZE = 128
EPS = 1e-6
NEG_INF = -1e30

LANES = 128
SUBLANES = 8
HALO_ROWS = 32
VMEM_LIMIT = 56 * 1024 * 1024
PAGES_PER_STEP = 4
BIAS_PAGES_PER_STEP = 16
N_SPLIT = 3


def _params(n_axes):
    return pltpu.CompilerParams(
        dimension_semantics=("arbitrary",) * n_axes,
        vmem_limit_bytes=VMEM_LIMIT)


def _dot(a, b):
    return jnp.dot(a, b, preferred_element_type=F32)


def _dot_nt(a, b):
    return lax.dot_general(a, b, (((1,), (1,)), ((), ())),
                           preferred_element_type=F32)


def _sigmoid(x):
    return jax.nn.sigmoid(x)


def _log_sigmoid(z):
    return -(jnp.maximum(-z, 0.0) + jnp.log1p(jnp.exp(-jnp.abs(z))))


def _rms_rows(x):
    ms = jnp.mean(x * x, axis=-1, keepdims=True)
    return x * lax.rsqrt(ms + EPS)


def _split3(x):
    hi = x.astype(BF16)
    r1 = x - hi.astype(F32)
    mid = r1.astype(BF16)
    lo = (r1 - mid.astype(F32)).astype(BF16)
    return hi, mid, lo


def _dot3(x, w_bf16):
    hi, mid, lo = _split3(x)
    return _dot(hi, w_bf16) + _dot(mid, w_bf16) + _dot(lo, w_bf16)


def _dot3_left(w_bf16, x):
    hi, mid, lo = _split3(x)
    return _dot(w_bf16, hi) + _dot(w_bf16, mid) + _dot(w_bf16, lo)


def _glu_kernel(x_ref, g_ref, w1_ref, w2_ref, u_ref, xn_ref):
    @pl.when(pl.program_id(1) == 0)
    def _():
        xn_ref[...] = (_rms_rows(x_ref[...]) * g_ref[...]).astype(BF16)

    xn = xn_ref[...]
    a1 = _dot(xn, w1_ref[...])
    a2 = _dot(xn, w2_ref[...])
    u_ref[...] = a1 * _sigmoid(a2)


def _glu_call(h, g, w_pw1, tm, tn):
    m, d = h.shape
    nj = d // tn
    return pl.pallas_call(
        _glu_kernel,
        grid=(m // tm, nj),
        in_specs=[
            pl.BlockSpec((tm, d), lambda i, j: (i, 0)),
            pl.BlockSpec((1, d), lambda i, j: (0, 0)),
            pl.BlockSpec((d, tn), lambda i, j: (0, j)),
            pl.BlockSpec((d, tn), lambda i, j: (0, j + nj)),
        ],
        out_specs=pl.BlockSpec((tm, tn), lambda i, j: (i, j)),
        out_shape=jax.ShapeDtypeStruct((m, d), F32),
        scratch_shapes=[pltpu.VMEM((tm, d), BF16)],
        compiler_params=_params(2),
        name="glu",
    )(h, g, w_pw1, w_pw1)


def _conva_kernel(u_ref, halo_ref, prev_ref, wdw_ref, bdw_ref, lng_ref, lnb_ref,
                  w2_ref, h_ref, o_ref, full_ref, c_ref, *, tm, tps, rb):
    i = pl.program_id(0)
    d = u_ref.shape[1]
    first = (i % tps) == 0
    full_ref[0:HALO_ROWS, :] = jnp.where(first, prev_ref[0], halo_ref[...])
    full_ref[HALO_ROWS:HALO_ROWS + tm, :] = u_ref[...]
    off = HALO_ROWS - (CONV_A_WIDTH - 1)
    for r0 in range(0, tm, rb):
        for c0 in range(0, d, LANES):
            acc = jnp.broadcast_to(bdw_ref[:, c0:c0 + LANES], (rb, LANES))
            for j in range(CONV_A_WIDTH):
                s = r0 + off + j
                acc = acc + (wdw_ref[j:j + 1, c0:c0 + LANES]
                             * full_ref[s:s + rb, c0:c0 + LANES])
            c_ref[r0:r0 + rb, c0:c0 + LANES] = acc
    c = c_ref[...]
    mu = jnp.mean(c, axis=-1, keepdims=True)
    xc = c - mu
    var = jnp.mean(xc * xc, axis=-1, keepdims=True)
    y = xc * lax.rsqrt(var + EPS) * lng_ref[...] + lnb_ref[...]
    a = (y * _sigmoid(y)).astype(BF16)
    o_ref[...] = h_ref[...] + _dot(a, w2_ref[...])


def _conva_call(u, prev32, w_dw, b_dw, ln_g, ln_b, w_pw2, h, tm, seq_len):
    m, d = u.shape
    tps = max(seq_len // tm, 1)
    rb = min(tm, 128)
    if tm % HALO_ROWS == 0:
        k = tm // HALO_ROWS
        halo_map = lambda i: (jnp.maximum(i * k - 1, 0), 0)
    else:
        assert tps == 1
        halo_map = lambda i: (0, 0)
    kern = functools.partial(_conva_kernel, tm=tm, tps=tps, rb=rb)
    return pl.pallas_call(
        kern,
        grid=(m // tm,),
        in_specs=[
            pl.BlockSpec((tm, d), lambda i: (i, 0)),
            pl.BlockSpec((HALO_ROWS, d), halo_map),
            pl.BlockSpec((1, HALO_ROWS, d), lambda i: (i // tps, 0, 0)),
            pl.BlockSpec((CONV_A_WIDTH, d), lambda i: (0, 0)),
            pl.BlockSpec((1, d), lambda i: (0, 0)),
            pl.BlockSpec((1, d), lambda i: (0, 0)),
            pl.BlockSpec((1, d), lambda i: (0, 0)),
            pl.BlockSpec((d, d), lambda i: (0, 0)),
            pl.BlockSpec((tm, d), lambda i: (i, 0)),
        ],
        out_specs=pl.BlockSpec((tm, d), lambda i: (i, 0)),
        out_shape=jax.ShapeDtypeStruct((m, d), F32),
        scratch_shapes=[pltpu.VMEM((HALO_ROWS + tm, d), F32),
                        pltpu.VMEM((tm, d), F32)],
        compiler_params=_params(1),
        name="conv_a",
    )(u, u, prev32, w_dw, b_dw, ln_g, ln_b, w_pw2, h)


def _ffn_body(h_ref, g_ref, wu_ref, wg_ref, wdw_ref, bdw_ref, wd_ref,
              o_ref, xn_ref, ubuf_ref, tm, prev_rows, fix_shifted):
    xn = xn_ref[...]
    u = _dot(xn, wu_ref[...])
    ubuf_ref[SUBLANES:SUBLANES + tm, :] = u
    ubuf_ref[SUBLANES - 2:SUBLANES, :] = prev_rows
    u1 = ubuf_ref[SUBLANES - 1:SUBLANES - 1 + tm, :]
    u2 = ubuf_ref[SUBLANES - 2:SUBLANES - 2 + tm, :]
    u1, u2 = fix_shifted(u1, u2)
    c = (wdw_ref[0:1, :] * u2 + wdw_ref[1:2, :] * u1 + wdw_ref[2:3, :] * u
         + bdw_ref[...])
    gt = _dot(xn, wg_ref[...])
    act = (c * _sigmoid(c) * gt).astype(BF16)
    o_ref[...] += _dot(act, wd_ref[...])


def _ffn_seq_kernel(h_ref, g_ref, wu_ref, wg_ref, wdw_ref, bdw_ref, wd_ref,
                    prev_ref, o_ref, fb_ref, xn_ref, ubuf_ref, carry_ref,
                    *, tm, tps):
    i = pl.program_id(0)
    j = pl.program_id(1)
    first = (i % tps) == 0

    @pl.when(j == 0)
    def _():
        xn_ref[...] = (_rms_rows(h_ref[...]) * g_ref[...]).astype(BF16)
        o_ref[...] = h_ref[...]

        @pl.when(i == 0)
        def _():
            carry_ref[...] = jnp.zeros(carry_ref.shape, F32)

    prev_rows = jnp.where(first, prev_ref[0], carry_ref[j])
    _ffn_body(h_ref, g_ref, wu_ref, wg_ref, wdw_ref, bdw_ref, wd_ref,
              o_ref, xn_ref, ubuf_ref, tm, prev_rows, lambda a, b: (a, b))
    tail = ubuf_ref[SUBLANES + tm - 2:SUBLANES + tm, :]
    carry_ref[j] = tail

    @pl.when((i % tps) == tps - 1)
    def _():
        fb_ref[i // tps, j] = tail


def _ffn_multi_kernel(h_ref, g_ref, wu_ref, wg_ref, wdw_ref, bdw_ref, wd_ref,
                      p1_ref, p2_ref, o_ref, u_ref, xn_ref, ubuf_ref,
                      *, tm, seq_len):
    @pl.when(pl.program_id(1) == 0)
    def _():
        xn_ref[...] = (_rms_rows(h_ref[...]) * g_ref[...]).astype(BF16)
        o_ref[...] = h_ref[...]
        ubuf_ref[0:SUBLANES, :] = jnp.zeros((SUBLANES, ubuf_ref.shape[1]), F32)

    def fix_shifted(u1, u2):
        t = lax.broadcasted_iota(jnp.int32, (tm, 1), 0) % seq_len
        return (jnp.where(t >= 1, u1, p1_ref[...]),
                jnp.where(t >= 2, u2, p2_ref[...]))

    prev_rows = jnp.zeros((CONV_F_WIDTH - 1, ubuf_ref.shape[1]), F32)
    _ffn_body(h_ref, g_ref, wu_ref, wg_ref, wdw_ref, bdw_ref, wd_ref,
              o_ref, xn_ref, ubuf_ref, tm, prev_rows, fix_shifted)
    u_ref[...] = ubuf_ref[SUBLANES:SUBLANES + tm, :]


def _ffn_common_specs(tm, tn, d, f):
    return [
        pl.BlockSpec((tm, d), lambda i, j: (i, 0)),
        pl.BlockSpec((1, d), lambda i, j: (0, 0)),
        pl.BlockSpec((d, tn), lambda i, j: (0, j)),
        pl.BlockSpec((d, tn), lambda i, j: (0, j)),
        pl.BlockSpec((CONV_F_WIDTH, tn), lambda i, j: (0, j)),
        pl.BlockSpec((1, tn), lambda i, j: (0, j)),
        pl.BlockSpec((tn, d), lambda i, j: (j, 0)),
    ]


def _ffn_seq_call(h, g, w_up, w_gate, w_dw, b_dw, w_down, prev, tm, tn, seq_len):
    m, d = h.shape
    f = w_up.shape[1]
    n_seq = m // seq_len
    tps = seq_len // tm
    nj = f // tn
    kern = functools.partial(_ffn_seq_kernel, tm=tm, tps=tps)
    h_new, fb = pl.pallas_call(
        kern,
        grid=(m // tm, nj),
        in_specs=_ffn_common_specs(tm, tn, d, f) + [
            pl.BlockSpec((1, CONV_F_WIDTH - 1, tn), lambda i, j: (i // tps, 0, j)),
        ],
        out_specs=[
            pl.BlockSpec((tm, d), lambda i, j: (i, 0)),
            pl.BlockSpec((n_seq, nj, CONV_F_WIDTH - 1, tn), lambda i, j: (0, 0, 0, 0)),
        ],
        out_shape=[jax.ShapeDtypeStruct((m, d), F32),
                   jax.ShapeDtypeStruct((n_seq, nj, CONV_F_WIDTH - 1, tn), F32)],
        scratch_shapes=[pltpu.VMEM((tm, d), BF16),
                        pltpu.VMEM((SUBLANES + tm, tn), F32),
                        pltpu.VMEM((nj, CONV_F_WIDTH - 1, tn), F32)],
        compiler_params=_params(2),
        name="ffn_seq",
    )(h, g, w_up, w_gate, w_dw, b_dw, w_down, prev)
    fb = jnp.transpose(fb, (0, 2, 1, 3)).reshape(n_seq, CONV_F_WIDTH - 1, f)
    return h_new, fb


def _ffn_multi_call(h, g, w_up, w_gate, w_dw, b_dw, w_down, p1, p2, tn, seq_len):
    m, d = h.shape
    f = w_up.shape[1]
    kern = functools.partial(_ffn_multi_kernel, tm=m, seq_len=seq_len)
    return pl.pallas_call(
        kern,
        grid=(1, f // tn),
        in_specs=_ffn_common_specs(m, tn, d, f) + [
            pl.BlockSpec((m, tn), lambda i, j: (0, j)),
            pl.BlockSpec((m, tn), lambda i, j: (0, j)),
        ],
        out_specs=[
            pl.BlockSpec((m, d), lambda i, j: (0, 0)),
            pl.BlockSpec((m, tn), lambda i, j: (0, j)),
        ],
        out_shape=[jax.ShapeDtypeStruct((m, d), F32),
                   jax.ShapeDtypeStruct((m, f), F32)],
        scratch_shapes=[pltpu.VMEM((m, d), BF16),
                        pltpu.VMEM((SUBLANES + m, tn), F32)],
        compiler_params=_params(2),
        name="ffn_multi",
    )(h, g, w_up, w_gate, w_dw, b_dw, w_down, p1, p2)


def _ple_kernel(h_ref, hj_ref, g_ref, p_ref, wg_ref, wp_ref, o_ref, xn_ref, pb_ref):
    @pl.when(pl.program_id(1) == 0)
    def _():
        xn_ref[...] = (_rms_rows(h_ref[...]) * g_ref[...]).astype(BF16)
        pb_ref[...] = p_ref[...].astype(BF16)

    gate = _dot(xn_ref[...], wg_ref[...])
    proj = _dot(pb_ref[...], wp_ref[...])
    o_ref[...] = hj_ref[...] + _sigmoid(gate) * proj


def _ple_call(h, g, p, w_gate, w_proj, tm, tn):
    m, d = h.shape
    pd = p.shape[1]
    return pl.pallas_call(
        _ple_kernel,
        grid=(m // tm, d // tn),
        in_specs=[
            pl.BlockSpec((tm, d), lambda i, j: (i, 0)),
            pl.BlockSpec((tm, tn), lambda i, j: (i, j)),
            pl.BlockSpec((1, d), lambda i, j: (0, 0)),
            pl.BlockSpec((tm, pd), lambda i, j: (i, 0)),
            pl.BlockSpec((d, tn), lambda i, j: (0, j)),
            pl.BlockSpec((pd, tn), lambda i, j: (0, j)),
        ],
        out_specs=pl.BlockSpec((tm, tn), lambda i, j: (i, j)),
        out_shape=jax.ShapeDtypeStruct((m, d), F32),
        scratch_shapes=[pltpu.VMEM((tm, d), BF16), pltpu.VMEM((tm, pd), BF16)],
        compiler_params=_params(2),
        name="ple",
    )(h, h, g, p, w_gate, w_proj)


def _qkv_kernel(h_ref, gkv_ref, gmix_ref, wk_ref, wv_ref, wq_ref, gk_ref, gq_ref,
                wf_ref, bf_ref, tri_ref, place_ref, *refs, tm, tn, tps, transposed):
    if transposed:
        (k_ref, v_ref, kb_ref, kx_ref, vt_ref, qt_ref, lf_ref,
         xkv_ref, xq_ref, carry_ref, cp_ref) = refs
    else:
        (k_ref, v_ref, kb_ref, vb_ref, q_ref, lf_ref, c_ref,
         xkv_ref, xq_ref, carry_ref) = refs
    i = pl.program_id(0)
    j = pl.program_id(1)

    @pl.when(j == 0)
    def _():
        r = _rms_rows(h_ref[...])
        xkv = (r * gkv_ref[...]).astype(BF16)
        xkv_ref[...] = xkv
        xq_ref[...] = (r * gmix_ref[...]).astype(BF16)
        lf = _log_sigmoid(_dot(xkv, wf_ref[...]) + bf_ref[...])
        lf_ref[...] = lf

        @pl.when((i % tps) == 0)
        def _():
            carry_ref[...] = jnp.zeros(carry_ref.shape, F32)

        c = _dot3_left(tri_ref[...], lf) + carry_ref[...]
        carry_ref[...] = c[tm - 1:tm, :]
        if transposed:
            for piece, val in enumerate(_split3(-c)):
                cp_ref[piece] = val
        else:
            c_ref[...] = c

    xkv = xkv_ref[...]
    kk = _dot(xkv, wk_ref[...])
    vv = _dot(xkv, wv_ref[...])
    qq = _dot(xq_ref[...], wq_ref[...])
    v_ref[...] = vv
    scale = HEAD_DIM ** -0.5
    qn_heads = []
    for c0 in range(0, tn, HEAD_DIM):
        sl = slice(c0, c0 + HEAD_DIM)
        kn = _rms_rows(kk[:, sl]) * gk_ref[...]
        k_ref[:, sl] = kn
        kb_ref[:, sl] = kn.astype(BF16)
        qn_heads.append(_rms_rows(qq[:, sl]) * gq_ref[...] * scale)
    qn = jnp.concatenate(qn_heads, axis=1)
    if transposed:
        vt_ref[0] = vv.T.astype(BF16)
        qt_ref[0] = qn.T.astype(BF16)
        kx = _dot(cp_ref[0], place_ref[0])
        for piece in range(1, N_SPLIT):
            kx = kx + _dot(cp_ref[piece], place_ref[piece])
        kx_ref[...] = kx.astype(BF16)
    else:
        vb_ref[...] = vv.astype(BF16)
        q_ref[...] = qn.astype(BF16)


def _qkv_call(h, g_kv, g_mix, w_k, w_v, w_q, g_k, g_q, w_f_pad, b_f_pad,
              tri, place, tm, tn, seq_len, transposed):
    m, d = h.shape
    tps = max(seq_len // tm, 1)
    kern = functools.partial(_qkv_kernel, tm=tm, tn=tn, tps=tps, transposed=transposed)
    row = lambda i, j: (i, 0)
    const = lambda i, j: (0, 0)
    col = lambda i, j: (0, j)
    tile = lambda i, j: (i, j)
    tile_t = lambda i, j: (i, j, 0)
    in_specs = [
        pl.BlockSpec((tm, d), row),
        pl.BlockSpec((1, d), const),
        pl.BlockSpec((1, d), const),
        pl.BlockSpec((d, tn), col),
        pl.BlockSpec((d, tn), col),
        pl.BlockSpec((d, tn), col),
        pl.BlockSpec((1, HEAD_DIM), const),
        pl.BlockSpec((1, HEAD_DIM), const),
        pl.BlockSpec((d, LANES), const),
        pl.BlockSpec((1, LANES), const),
        pl.BlockSpec((tm, tm), const),
        pl.BlockSpec((N_SPLIT, LANES, tn), lambda i, j: (0, 0, j)),
    ]
    scratch = [pltpu.VMEM((tm, d), BF16), pltpu.VMEM((tm, d), BF16),
               pltpu.VMEM((1, LANES), F32)]
    f32_md = jax.ShapeDtypeStruct((m, d), F32)
    bf_md = jax.ShapeDtypeStruct((m, d), BF16)
    if transposed:
        out_specs = [
            pl.BlockSpec((tm, tn), tile), pl.BlockSpec((tm, tn), tile),
            pl.BlockSpec((tm, tn), tile), pl.BlockSpec((tm, tn), tile),
            pl.BlockSpec((1, tn, tm), tile_t), pl.BlockSpec((1, tn, tm), tile_t),
            pl.BlockSpec((tm, LANES), row),
        ]
        bf_t = jax.ShapeDtypeStruct((m // tm, d, tm), BF16)
        out_shape = [f32_md, f32_md, bf_md, bf_md, bf_t, bf_t,
                     jax.ShapeDtypeStruct((m, LANES), F32)]
        scratch.append(pltpu.VMEM((N_SPLIT, tm, LANES), BF16))
    else:
        out_specs = [
            pl.BlockSpec((tm, tn), tile), pl.BlockSpec((tm, tn), tile),
            pl.BlockSpec((tm, tn), tile), pl.BlockSpec((tm, tn), tile),
            pl.BlockSpec((tm, tn), tile),
            pl.BlockSpec((tm, LANES), row), pl.BlockSpec((tm, LANES), row),
        ]
        out_shape = [f32_md, f32_md, bf_md, bf_md, bf_md,
                     jax.ShapeDtypeStruct((m, LANES), F32),
                     jax.ShapeDtypeStruct((m, LANES), F32)]
    return pl.pallas_call(
        kern,
        grid=(m // tm, d // tn),
        in_specs=in_specs,
        out_specs=out_specs,
        out_shape=out_shape,
        scratch_shapes=scratch,
        compiler_params=_params(2),
        name="qkv_t" if transposed else "qkv",
    )(h, g_kv, g_mix, w_k, w_v, w_q, g_k, g_q, w_f_pad, b_f_pad, tri, place)


def _attn_kernel(q_ref, k_ref, kx_ref, v_ref, o_ref, acc_ref, *, tq):
    qi = pl.program_id(2)
    ones_rows = (lax.broadcasted_iota(jnp.int32, (HEAD_DIM, tq), 0) < N_SPLIT)
    q_aug = jnp.concatenate([q_ref[0], ones_rows.astype(BF16)], axis=0)
    acc_ref[...] = jnp.zeros(acc_ref.shape, F32)

    def block(kb, m_old, l_old, diagonal):
        start = pl.multiple_of(kb * tq, tq)
        k_aug = jnp.concatenate([k_ref[pl.ds(start, tq), :],
                                 kx_ref[pl.ds(start, tq), :]], axis=1)
        s = _dot(k_aug, q_aug)
        if diagonal:
            key = lax.broadcasted_iota(jnp.int32, (tq, tq), 0)
            qry = lax.broadcasted_iota(jnp.int32, (tq, tq), 1)
            s = jnp.where(key <= qry, s, NEG_INF)
        m_new = jnp.maximum(m_old, jnp.max(s, axis=0, keepdims=True))
        alpha = jnp.exp(m_old - m_new)
        p = jnp.exp(s - m_new)
        l_new = alpha * l_old + jnp.sum(p, axis=0, keepdims=True)
        acc_ref[...] = alpha * acc_ref[...] + _dot(v_ref[kb], p.astype(BF16))
        return m_new, l_new

    m0 = jnp.full((1, tq), NEG_INF, F32)
    l0 = jnp.zeros((1, tq), F32)
    m, l = lax.fori_loop(0, qi, lambda kb, ml: block(kb, ml[0], ml[1], False), (m0, l0))
    m, l = block(qi, m, l, True)
    o_ref[...] = (acc_ref[...] / l).T.astype(o_ref.dtype)


def _attn_call(q_t, k, kx, v_t, n_seq, seq_len, tq):
    d = k.shape[1]
    nh = d // HEAD_DIM
    nb = seq_len // tq
    kern = functools.partial(_attn_kernel, tq=tq)
    return pl.pallas_call(
        kern,
        grid=(n_seq, nh, nb),
        in_specs=[
            pl.BlockSpec((1, HEAD_DIM, tq), lambda n, h, i: (n * nb + i, h, 0)),
            pl.BlockSpec((seq_len, HEAD_DIM), lambda n, h, i: (n, h)),
            pl.BlockSpec((seq_len, HEAD_DIM), lambda n, h, i: (n, h)),
            pl.BlockSpec((nb, HEAD_DIM, tq), lambda n, h, i: (n, h, 0)),
        ],
        out_specs=pl.BlockSpec((tq, HEAD_DIM), lambda n, h, i: (n * nb + i, h)),
        out_shape=jax.ShapeDtypeStruct((n_seq * seq_len, d), BF16),
        scratch_shapes=[pltpu.VMEM((HEAD_DIM, tq), F32)],
        compiler_params=_params(3),
        name="fox_prompt",
    )(q_t, k, kx, v_t)


def _bias_kernel(pt_ref, *refs, n_steps):
    nb = BIAS_PAGES_PER_STEP
    lf_refs = refs[0:nb]
    sfx_ref, later_ref, spread_ref, o_ref, x_ref = refs[nb:]
    s = pl.program_id(1)
    base = pl.multiple_of(s * nb, nb)
    for e in range(nb):
        x_ref[pl.ds(base + e, 1), :] = lf_refs[e][0]

    @pl.when(s == n_steps - 1)
    def _():
        x = x_ref[...]
        within = _dot3(x, sfx_ref[...])
        tot = (within + x)[:, 0:LANES]
        later = _dot3_left(later_ref[...], tot)
        o_ref[0] = within + _dot3(later, spread_ref[...])


def _bias_call(page_table, lf_flat, sfx, later_m, spread):
    b, n_pages = page_table.shape
    width = lf_flat.shape[2]
    nb = BIAS_PAGES_PER_STEP
    n_steps = n_pages // nb

    def page_map(e):
        return lambda bi, s, pt: (pt[bi, s * nb + e], 0, 0)

    const2 = lambda bi, s, pt: (0, 0)
    in_specs = [pl.BlockSpec((1, 1, width), page_map(e)) for e in range(nb)]
    in_specs += [pl.BlockSpec(sfx.shape, const2), pl.BlockSpec(later_m.shape, const2),
                 pl.BlockSpec(spread.shape, const2)]
    grid_spec = pltpu.PrefetchScalarGridSpec(
        num_scalar_prefetch=1,
        grid=(b, n_steps),
        in_specs=in_specs,
        out_specs=pl.BlockSpec((1, n_pages, width), lambda bi, s, pt: (bi, 0, 0)),
        scratch_shapes=[pltpu.VMEM((n_pages, width), F32)],
    )
    return pl.pallas_call(
        functools.partial(_bias_kernel, n_steps=n_steps),
        grid_spec=grid_spec,
        out_shape=jax.ShapeDtypeStruct((b, n_pages, width), F32),
        compiler_params=_params(2),
        name="decode_bias",
    )(page_table, *([lf_flat] * nb), sfx, later_m, spread)


def _sattn_kernel(pt_ref, q_ref, *refs, n_groups):
    g_pages = PAGES_PER_STEP
    k_refs = refs[0:g_pages]
    v_refs = refs[g_pages:2 * g_pages]
    (bias_ref, mask_ref, kn_ref, vn_ref, bn_ref, maskn_ref, o_ref,
     m_ref, l_ref, acc_ref) = refs[2 * g_pages:]
    g = pl.program_id(1)

    @pl.when(g == 0)
    def _():
        m_ref[...] = jnp.full(m_ref.shape, NEG_INF, F32)
        l_ref[...] = jnp.zeros(l_ref.shape, F32)
        acc_ref[...] = jnp.zeros(acc_ref.shape, F32)

    q = q_ref[0]

    def update(s, v_rows):
        m_old = m_ref[...]
        m_new = jnp.maximum(m_old, jnp.max(s, axis=-1, keepdims=True))
        alpha = jnp.exp(m_old - m_new)
        p = jnp.exp(s - m_new)
        l_ref[...] = alpha * l_ref[...] + jnp.sum(p, axis=-1, keepdims=True)
        acc_ref[...] = alpha * acc_ref[...] + _dot(p.astype(BF16), v_rows)
        m_ref[...] = m_new

    groups_per_block = SUBLANES // g_pages
    for e in range(g_pages):
        row = bias_ref[0, pl.ds((g % groups_per_block) * g_pages + e, 1), :]
        s = _dot_nt(q, k_refs[e][0].astype(BF16)) + (mask_ref[...] + row)
        update(s, v_refs[e][0].astype(BF16))

    @pl.when(g == n_groups - 1)
    def _():
        s = _dot_nt(q, kn_ref[0]) + (maskn_ref[...] + bn_ref[0])
        update(s, vn_ref[0])
        o_ref[0] = (acc_ref[...] / l_ref[...]).astype(o_ref.dtype)


def _sattn_call(page_table, q2, cache_k, cache_v, bias, mask, k_new, v_new,
                bias_new, mask_new):
    b, rows, dh = q2.shape
    n_pages = page_table.shape[1]
    n_groups = n_pages // PAGES_PER_STEP
    page_rows = cache_k.shape[1]
    groups_per_block = SUBLANES // PAGES_PER_STEP

    def page_map(e):
        return lambda bi, g, pt: (pt[bi, g * PAGES_PER_STEP + e], 0, 0)

    per_seq = lambda bi, g, pt: (bi, 0, 0)
    const2 = lambda bi, g, pt: (0, 0)
    in_specs = [pl.BlockSpec((1, rows, dh), per_seq)]
    in_specs += [pl.BlockSpec((1, page_rows, dh), page_map(e)) for e in range(PAGES_PER_STEP)]
    in_specs += [pl.BlockSpec((1, page_rows, dh), page_map(e)) for e in range(PAGES_PER_STEP)]
    in_specs += [
        pl.BlockSpec((1, SUBLANES, page_rows), lambda bi, g, pt: (bi, g // groups_per_block, 0)),
        pl.BlockSpec(mask.shape, const2),
        pl.BlockSpec((1, rows, dh), per_seq),
        pl.BlockSpec((1, rows, dh), per_seq),
        pl.BlockSpec((1, 1, rows), per_seq),
        pl.BlockSpec(mask_new.shape, const2),
    ]
    grid_spec = pltpu.PrefetchScalarGridSpec(
        num_scalar_prefetch=1,
        grid=(b, n_groups),
        in_specs=in_specs,
        out_specs=pl.BlockSpec((1, rows, dh), per_seq),
        scratch_shapes=[pltpu.VMEM((rows, 1), F32), pltpu.VMEM((rows, 1), F32),
                        pltpu.VMEM((rows, dh), F32)],
    )
    kern = functools.partial(_sattn_kernel, n_groups=n_groups)
    return pl.pallas_call(
        kern,
        grid_spec=grid_spec,
        out_shape=jax.ShapeDtypeStruct((b, rows, dh), BF16),
        compiler_params=_params(2),
        name="fox_decode",
    )(page_table, q2, *([cache_k] * PAGES_PER_STEP), *([cache_v] * PAGES_PER_STEP),
      bias, mask, k_new, v_new, bias_new, mask_new)


def _oproj_kernel(a_ref, w_ref, h_ref, o_ref):
    o_ref[...] = h_ref[...] + _dot(a_ref[...], w_ref[...])


def _oproj_call(a, w_o, h, tm, tn):
    m, d = h.shape
    return pl.pallas_call(
        _oproj_kernel,
        grid=(m // tm, d // tn),
        in_specs=[
            pl.BlockSpec((tm, d), lambda i, j: (i, 0)),
            pl.BlockSpec((d, tn), lambda i, j: (0, j)),
            pl.BlockSpec((tm, tn), lambda i, j: (i, j)),
        ],
        out_specs=pl.BlockSpec((tm, tn), lambda i, j: (i, j)),
        out_shape=jax.ShapeDtypeStruct((m, d), F32),
        compiler_params=_params(2),
        name="o_proj",
    )(a, w_o, h)


def _cumsum_matrix(n, group):
    t = jnp.arange(n)[:, None]
    tp = jnp.arange(n)[None, :]
    return ((tp <= t) & (tp // group == t // group)).astype(BF16)


def _placement(nh, d):
    lane = jnp.arange(LANES)[None, :, None]
    col = jnp.arange(d)[None, None, :]
    piece = jnp.arange(N_SPLIT)[:, None, None]
    return ((lane < nh) & (col == lane * HEAD_DIM + piece)).astype(BF16)


def kernel(x_prompt, x_sample, state_conv_a, state_ffn, cache_k, cache_v, cache_logf, page_table, p_prompt, p_sample, g_mix_a, w_pw1, w_dwa, b_dwa, ln_g, ln_b, w_pw2, g_kv, w_k, w_v, g_k, w_f, b_f, g_mix_b, w_q, g_q, w_o, g_ffn, w_up, w_gate, w_fdw, b_fdw, w_down, g_ple, w_ple_gate, w_ple_proj):
    n_p, t_p, d = x_prompt.shape
    n_s, t_s, _ = x_sample.shape
    depth = g_ffn.shape[0]
    n_a = g_mix_a.shape[0]
    f = w_up.shape[2]
    nh = d // HEAD_DIM
    m_s = n_s * t_s

    bf = lambda w: w.astype(BF16)
    w_pw1_b, w_pw2_b = bf(w_pw1), bf(w_pw2)
    w_k_b, w_v_b, w_q_b, w_o_b = bf(w_k), bf(w_v), bf(w_q), bf(w_o)
    w_up_b, w_gate_b, w_down_b = bf(w_up), bf(w_gate), bf(w_down)
    w_pg_b, w_pp_b = bf(w_ple_gate), bf(w_ple_proj)
    w_f_pad = jnp.pad(bf(w_f), ((0, 0), (0, LANES - nh)))
    b_f_pad = jnp.pad(b_f, (0, LANES - nh)).reshape(1, LANES)
    place = _placement(nh, d)
    row = lambda a: a.reshape(1, -1)

    def trunk(x, p, n_seq, seq_len, a_prev32, tm, tn, tm_conv, ffn, attend, tri,
              transposed):
        h = x.reshape(n_seq * seq_len, d)
        a_us, f_bufs = [], []
        k = v = lf = None
        for i in range(depth):
            if i < n_a:
                u = _glu_call(h, row(g_mix_a[i]), w_pw1_b[i], tm, tn)
                h = _conva_call(u, a_prev32[i], w_dwa[i], row(b_dwa[i]), row(ln_g[i]),
                                row(ln_b[i]), w_pw2_b[i], h, tm_conv, seq_len)
                a_us.append(u.reshape(n_seq, seq_len, d))
            else:
                jb = i - n_a
                outs = _qkv_call(
                    h, row(g_kv), row(g_mix_b[jb]), w_k_b, w_v_b, w_q_b[jb],
                    row(g_k), row(g_q[jb]), w_f_pad, b_f_pad, tri, place,
                    tm, tn, seq_len, transposed)
                k, v = outs[0], outs[1]
                lf = outs[6] if transposed else outs[5]
                a = attend(outs)
                h = _oproj_call(a, w_o_b[jb], h, tm, tn)
            h, fb = ffn(i, h)
            f_bufs.append(fb)
            h = _ple_call(h, row(g_ple[i]), p[i].reshape(n_seq * seq_len, -1),
                          w_pg_b[i], w_pp_b[i], tm, tn)
        shp = (n_seq, seq_len, nh, HEAD_DIM)
        return (h.reshape(n_seq, seq_len, d), a_us, jnp.stack(f_bufs),
                k.reshape(shp), v.reshape(shp),
                lf[:, :nh].reshape(n_seq, seq_len, nh))

    tm_p, tn_p = 512, 512

    def ffn_prompt(i, h):
        prev = jnp.zeros((n_p, CONV_F_WIDTH - 1, f), F32)
        return _ffn_seq_call(h, row(g_ffn[i]), w_up_b[i], w_gate_b[i], w_fdw[i],
                             row(b_fdw[i]), w_down_b[i], prev, tm_p, tn_p, t_p)

    def attend_prompt(outs):
        _, _, kb, kx, v_t, q_t, _ = outs
        return _attn_call(q_t, kb, kx, v_t, n_p, t_p, tm_p)

    a_prev_p = [jnp.zeros((n_p, HALO_ROWS, d), F32) for _ in range(n_a)]
    (y_p, a_us_p, ffn_p, k_p, v_p, lf_p) = trunk(
        x_prompt, p_prompt, n_p, t_p, a_prev_p, tm_p, tn_p, 256, ffn_prompt,
        attend_prompt, _cumsum_matrix(tm_p, tm_p), True)
    conv_a_p = jnp.stack([u[:, t_p - (CONV_A_WIDTH - 1):] for u in a_us_p])

    def ffn_sample(i, h):
        st = state_ffn[i]
        z = jnp.zeros((n_s, t_s, f), F32)
        p1 = z.at[:, 0].set(st[:, 1]).reshape(m_s, f)
        p2 = z.at[:, 0].set(st[:, 0]).at[:, 1].set(st[:, 1]).reshape(m_s, f)
        h, u = _ffn_multi_call(h, row(g_ffn[i]), w_up_b[i], w_gate_b[i], w_fdw[i],
                               row(b_fdw[i]), w_down_b[i], p1, p2, tn_p, t_s)
        full = jnp.concatenate([st, u.reshape(n_s, t_s, f)], axis=1)
        return h, full[:, -(CONV_F_WIDTH - 1):]

    n_pool = cache_k.shape[0]
    n_pages = page_table.shape[1]
    page_rows = PAGE_SIZE * nh
    cache_k2 = cache_k.reshape(n_pool, page_rows, HEAD_DIM)
    cache_v2 = cache_v.reshape(n_pool, page_rows, HEAD_DIM)
    lf_flat = cache_logf.reshape(n_pool, 1, page_rows)
    fr = jnp.arange(page_rows)
    same_head = (fr[:, None] % nh) == (fr[None, :] % nh)
    sfx = (same_head & (fr[:, None] // nh > fr[None, :] // nh)).astype(BF16)
    pg = jnp.arange(n_pages)
    later_m = (pg[None, :] > pg[:, None]).astype(BF16)
    spread = ((jnp.arange(LANES)[:, None] == (fr[None, :] % nh))
              & (jnp.arange(LANES)[:, None] < nh)).astype(BF16)
    qrow = jnp.arange(t_s * nh)
    mask = jnp.where((qrow[:, None] % nh) == (fr[None, :] % nh), 0.0, NEG_INF).astype(F32)
    mask_new = jnp.where(((qrow[:, None] % nh) == (qrow[None, :] % nh))
                         & (qrow[None, :] // nh <= qrow[:, None] // nh),
                         0.0, NEG_INF).astype(F32)
    bias = _bias_call(page_table, lf_flat, sfx, later_m, spread)

    def attend_sample(outs):
        _, _, kb, vb, q, _, c = outs
        per_seq = lambda a: a.reshape(n_s, t_s * nh, HEAD_DIM)
        bias_new = -c[:, :nh].reshape(n_s, 1, t_s * nh)
        a = _sattn_call(page_table, per_seq(q), cache_k2, cache_v2, bias, mask,
                        per_seq(kb), per_seq(vb), bias_new, mask_new)
        return a.reshape(m_s, d)

    a_prev_s = [jnp.pad(state_conv_a[i], ((0, 0), (HALO_ROWS - (CONV_A_WIDTH - 1), 0), (0, 0)))
                for i in range(n_a)]
    (y_s, a_us_s, ffn_s, k_s, v_s, lf_s) = trunk(
        x_sample, p_sample, n_s, t_s, a_prev_s, m_s, tn_p, t_s, ffn_sample,
        attend_sample, _cumsum_matrix(m_s, t_s), False)
    conv_a_s = jnp.stack([
        jnp.concatenate([state_conv_a[i], a_us_s[i]], axis=1)[:, -(CONV_A_WIDTH - 1):]
        for i in range(n_a)])

    return (y_p, y_s, conv_a_p, ffn_p, k_p, v_p, lf_p,
            conv_a_s, ffn_s, k_s, v_s, lf_s)
```

```python
import functools
from typing import NamedTuple

import jax
import jax.numpy as jnp
from jax import lax
from jax.experimental import pallas as pl
from jax.experimental.pallas import tpu as pltpu

F32 = jnp.float32
BF16 = jnp.bfloat16

HEAD_DIM = 128
CONV_A_WIDTH = 31
CONV_F_WIDTH = 3
PAGE_SIZE = 128
EPS = 1e-6
NEG_INF = -1e30
LOG2_E = 1.4426950408889634

LANES = 128
SUBLANES = 8
HALO_ROWS = 32
VMEM_LIMIT = 56 * 1024 * 1024
PAGES_PER_STEP = 4
BIAS_PAGES_PER_STEP = 16
N_SPLIT = 3


class _Tiles(NamedTuple):
    rows: int
    rows_qkv: int
    rows_ffn: int
    rows_conv: int
    cols: int
    cols_oproj: int


PROMPT_TILES = _Tiles(rows=1024, rows_qkv=512, rows_ffn=512, rows_conv=256,
                      cols=512, cols_oproj=1024)


def _params(n_axes):
    return pltpu.CompilerParams(
        dimension_semantics=("arbitrary",) * n_axes,
        vmem_limit_bytes=VMEM_LIMIT)


def _dot(a, b):
    return jnp.dot(a, b, preferred_element_type=F32)


def _dot_nt(a, b):
    return lax.dot_general(a, b, (((1,), (1,)), ((), ())),
                           preferred_element_type=F32)


def _sigmoid(x):
    return jax.nn.sigmoid(x)


def _log_sigmoid(z):
    return -(jnp.maximum(-z, 0.0) + jnp.log1p(jnp.exp(-jnp.abs(z))))


def _rms_rows(x):
    ms = jnp.mean(x * x, axis=-1, keepdims=True)
    return x * lax.rsqrt(ms + EPS)


def _split3(x):
    hi = x.astype(BF16)
    r1 = x - hi.astype(F32)
    mid = r1.astype(BF16)
    lo = (r1 - mid.astype(F32)).astype(BF16)
    return hi, mid, lo


def _dot3(x, w_bf16):
    hi, mid, lo = _split3(x)
    return _dot(hi, w_bf16) + _dot(mid, w_bf16) + _dot(lo, w_bf16)


def _dot3_left(w_bf16, x):
    hi, mid, lo = _split3(x)
    return _dot(w_bf16, hi) + _dot(w_bf16, mid) + _dot(w_bf16, lo)


def _glu_kernel(x_ref, g_ref, w1_ref, w2_ref, u_ref, xn_ref):
    @pl.when(pl.program_id(1) == 0)
    def _():
        xn_ref[...] = (_rms_rows(x_ref[...]) * g_ref[...]).astype(BF16)

    xn = xn_ref[...]
    a1 = _dot(xn, w1_ref[...])
    a2 = _dot(xn, w2_ref[...])
    u_ref[...] = a1 * _sigmoid(a2)


def _glu_call(h, g, w_pw1, tm, tn):
    m, d = h.shape
    nj = d // tn
    return pl.pallas_call(
        _glu_kernel,
        grid=(m // tm, nj),
        in_specs=[
            pl.BlockSpec((tm, d), lambda i, j: (i, 0)),
            pl.BlockSpec((1, d), lambda i, j: (0, 0)),
            pl.BlockSpec((d, tn), lambda i, j: (0, j)),
            pl.BlockSpec((d, tn), lambda i, j: (0, j + nj)),
        ],
        out_specs=pl.BlockSpec((tm, tn), lambda i, j: (i, j)),
        out_shape=jax.ShapeDtypeStruct((m, d), F32),
        scratch_shapes=[pltpu.VMEM((tm, d), BF16)],
        compiler_params=_params(2),
        name="glu",
    )(h, g, w_pw1, w_pw1)


def _conva_kernel(u_ref, halo_ref, prev_ref, wdw_ref, bdw_ref, lng_ref, lnb_ref,
                  w2_ref, h_ref, o_ref, full_ref, c_ref, *, tm, tps, rb):
    i = pl.program_id(0)
    d = u_ref.shape[1]
    first = (i % tps) == 0
    full_ref[0:HALO_ROWS, :] = jnp.where(first, prev_ref[0], halo_ref[...])
    full_ref[HALO_ROWS:HALO_ROWS + tm, :] = u_ref[...]
    full_ref[HALO_ROWS + tm:HALO_ROWS + tm + SUBLANES, :] = jnp.zeros((SUBLANES, d), F32)
    off = HALO_ROWS - (CONV_A_WIDTH - 1)
    for r0 in range(0, tm, rb):
        for c0 in range(0, d, LANES):
            cols = slice(c0, c0 + LANES)
            acc = jnp.broadcast_to(bdw_ref[:, cols], (rb, LANES))
            for b in range(SUBLANES):
                z = None
                for j in range(CONV_A_WIDTH):
                    if (off + j) % SUBLANES != b:
                        continue
                    s = r0 + (off + j) // SUBLANES * SUBLANES
                    term = wdw_ref[j:j + 1, cols] * full_ref[s:s + rb + SUBLANES, cols]
                    z = term if z is None else z + term
                acc = acc + z[b:b + rb, :]
            c_ref[r0:r0 + rb, cols] = acc
    c = c_ref[...]
    mu = jnp.mean(c, axis=-1, keepdims=True)
    xc = c - mu
    var = jnp.mean(xc * xc, axis=-1, keepdims=True)
    y = xc * lax.rsqrt(var + EPS) * lng_ref[...] + lnb_ref[...]
    a = (y * _sigmoid(y)).astype(BF16)
    o_ref[...] = h_ref[...] + _dot(a, w2_ref[...])


def _conva_call(u, prev32, w_dw, b_dw, ln_g, ln_b, w_pw2, h, tm, seq_len):
    m, d = u.shape
    tps = max(seq_len // tm, 1)
    rb = min(tm, 128)
    if tm % HALO_ROWS == 0:
        k = tm // HALO_ROWS
        halo_map = lambda i: (jnp.maximum(i * k - 1, 0), 0)
    else:
        assert tps == 1
        halo_map = lambda i: (0, 0)
    kern = functools.partial(_conva_kernel, tm=tm, tps=tps, rb=rb)
    return pl.pallas_call(
        kern,
        grid=(m // tm,),
        in_specs=[
            pl.BlockSpec((tm, d), lambda i: (i, 0)),
            pl.BlockSpec((HALO_ROWS, d), halo_map),
            pl.BlockSpec((1, HALO_ROWS, d), lambda i: (i // tps, 0, 0)),
            pl.BlockSpec((CONV_A_WIDTH, d), lambda i: (0, 0)),
            pl.BlockSpec((1, d), lambda i: (0, 0)),
            pl.BlockSpec((1, d), lambda i: (0, 0)),
            pl.BlockSpec((1, d), lambda i: (0, 0)),
            pl.BlockSpec((d, d), lambda i: (0, 0)),
            pl.BlockSpec((tm, d), lambda i: (i, 0)),
        ],
        out_specs=pl.BlockSpec((tm, d), lambda i: (i, 0)),
        out_shape=jax.ShapeDtypeStruct((m, d), F32),
        scratch_shapes=[pltpu.VMEM((HALO_ROWS + tm + SUBLANES, d), F32),
                        pltpu.VMEM((tm, d), F32)],
        compiler_params=_params(1),
        name="conv_a",
    )(u, u, prev32, w_dw, b_dw, ln_g, ln_b, w_pw2, h)


def _ffn_body(h_ref, g_ref, wu_ref, wg_ref, wdw_ref, bdw_ref, wd_ref,
              o_ref, xn_ref, ubuf_ref, tm, prev_rows, fix_shifted):
    xn = xn_ref[...]
    u = _dot(xn, wu_ref[...])
    ubuf_ref[SUBLANES:SUBLANES + tm, :] = u
    ubuf_ref[SUBLANES - 2:SUBLANES, :] = prev_rows
    u1 = ubuf_ref[SUBLANES - 1:SUBLANES - 1 + tm, :]
    u2 = ubuf_ref[SUBLANES - 2:SUBLANES - 2 + tm, :]
    u1, u2 = fix_shifted(u1, u2)
    c = (wdw_ref[0:1, :] * u2 + wdw_ref[1:2, :] * u1 + wdw_ref[2:3, :] * u
         + bdw_ref[...])
    gt = _dot(xn, wg_ref[...])
    act = (c * _sigmoid(c) * gt).astype(BF16)
    o_ref[...] += _dot(act, wd_ref[...])


def _ffn_seq_kernel(h_ref, g_ref, wu_ref, wg_ref, wdw_ref, bdw_ref, wd_ref,
                    prev_ref, o_ref, fb_ref, xn_ref, ubuf_ref, carry_ref,
                    *, tm, tps):
    i = pl.program_id(0)
    j = pl.program_id(1)
    first = (i % tps) == 0

    @pl.when(j == 0)
    def _():
        xn_ref[...] = (_rms_rows(h_ref[...]) * g_ref[...]).astype(BF16)
        o_ref[...] = h_ref[...]

        @pl.when(i == 0)
        def _():
            carry_ref[...] = jnp.zeros(carry_ref.shape, F32)

    prev_rows = jnp.where(first, prev_ref[0], carry_ref[j])
    _ffn_body(h_ref, g_ref, wu_ref, wg_ref, wdw_ref, bdw_ref, wd_ref,
              o_ref, xn_ref, ubuf_ref, tm, prev_rows, lambda a, b: (a, b))
    tail = ubuf_ref[SUBLANES + tm - 2:SUBLANES + tm, :]
    carry_ref[j] = tail

    @pl.when((i % tps) == tps - 1)
    def _():
        fb_ref[i // tps, j] = tail


def _ffn_multi_kernel(h_ref, g_ref, wu_ref, wg_ref, wdw_ref, bdw_ref, wd_ref,
                      p1_ref, p2_ref, o_ref, u_ref, xn_ref, ubuf_ref,
                      *, tm, seq_len):
    @pl.when(pl.program_id(1) == 0)
    def _():
        xn_ref[...] = (_rms_rows(h_ref[...]) * g_ref[...]).astype(BF16)
        o_ref[...] = h_ref[...]
        ubuf_ref[0:SUBLANES, :] = jnp.zeros((SUBLANES, ubuf_ref.shape[1]), F32)

    def fix_shifted(u1, u2):
        t = lax.broadcasted_iota(jnp.int32, (tm, 1), 0) % seq_len
        return (jnp.where(t >= 1, u1, p1_ref[...]),
                jnp.where(t >= 2, u2, p2_ref[...]))

    prev_rows = jnp.zeros((CONV_F_WIDTH - 1, ubuf_ref.shape[1]), F32)
    _ffn_body(h_ref, g_ref, wu_ref, wg_ref, wdw_ref, bdw_ref, wd_ref,
              o_ref, xn_ref, ubuf_ref, tm, prev_rows, fix_shifted)
    u_ref[...] = ubuf_ref[SUBLANES:SUBLANES + tm, :]


def _ffn_common_specs(tm, tn, d, f):
    return [
        pl.BlockSpec((tm, d), lambda i, j: (i, 0)),
        pl.BlockSpec((1, d), lambda i, j: (0, 0)),
        pl.BlockSpec((d, tn), lambda i, j: (0, j)),
        pl.BlockSpec((d, tn), lambda i, j: (0, j)),
        pl.BlockSpec((CONV_F_WIDTH, tn), lambda i, j: (0, j)),
        pl.BlockSpec((1, tn), lambda i, j: (0, j)),
        pl.BlockSpec((tn, d), lambda i, j: (j, 0)),
    ]


def _ffn_seq_call(h, g, w_up, w_gate, w_dw, b_dw, w_down, prev, tm, tn, seq_len):
    m, d = h.shape
    f = w_up.shape[1]
    n_seq = m // seq_len
    tps = seq_len // tm
    nj = f // tn
    kern = functools.partial(_ffn_seq_kernel, tm=tm, tps=tps)
    h_new, fb = pl.pallas_call(
        kern,
        grid=(m // tm, nj),
        in_specs=_ffn_common_specs(tm, tn, d, f) + [
            pl.BlockSpec((1, CONV_F_WIDTH - 1, tn), lambda i, j: (i // tps, 0, j)),
        ],
        out_specs=[
            pl.BlockSpec((tm, d), lambda i, j: (i, 0)),
            pl.BlockSpec((n_seq, nj, CONV_F_WIDTH - 1, tn), lambda i, j: (0, 0, 0, 0)),
        ],
        out_shape=[jax.ShapeDtypeStruct((m, d), F32),
                   jax.ShapeDtypeStruct((n_seq, nj, CONV_F_WIDTH - 1, tn), F32)],
        scratch_shapes=[pltpu.VMEM((tm, d), BF16),
                        pltpu.VMEM((SUBLANES + tm, tn), F32),
                        pltpu.VMEM((nj, CONV_F_WIDTH - 1, tn), F32)],
        compiler_params=_params(2),
        name="ffn_seq",
    )(h, g, w_up, w_gate, w_dw, b_dw, w_down, prev)
    fb = jnp.transpose(fb, (0, 2, 1, 3)).reshape(n_seq, CONV_F_WIDTH - 1, f)
    return h_new, fb


def _ffn_multi_call(h, g, w_up, w_gate, w_dw, b_dw, w_down, p1, p2, tn, seq_len):
    m, d = h.shape
    f = w_up.shape[1]
    kern = functools.partial(_ffn_multi_kernel, tm=m, seq_len=seq_len)
    return pl.pallas_call(
        kern,
        grid=(1, f // tn),
        in_specs=_ffn_common_specs(m, tn, d, f) + [
            pl.BlockSpec((m, tn), lambda i, j: (0, j)),
            pl.BlockSpec((m, tn), lambda i, j: (0, j)),
        ],
        out_specs=[
            pl.BlockSpec((m, d), lambda i, j: (0, 0)),
            pl.BlockSpec((m, tn), lambda i, j: (0, j)),
        ],
        out_shape=[jax.ShapeDtypeStruct((m, d), F32),
                   jax.ShapeDtypeStruct((m, f), F32)],
        scratch_shapes=[pltpu.VMEM((m, d), BF16),
                        pltpu.VMEM((SUBLANES + m, tn), F32)],
        compiler_params=_params(2),
        name="ffn_multi",
    )(h, g, w_up, w_gate, w_dw, b_dw, w_down, p1, p2)


def _ple_kernel(h_ref, hj_ref, g_ref, p_ref, wg_ref, wp_ref, o_ref, xn_ref, pb_ref):
    @pl.when(pl.program_id(1) == 0)
    def _():
        xn_ref[...] = (_rms_rows(h_ref[...]) * g_ref[...]).astype(BF16)
        pb_ref[...] = p_ref[...].astype(BF16)

    gate = _dot(xn_ref[...], wg_ref[...])
    proj = _dot(pb_ref[...], wp_ref[...])
    o_ref[...] = hj_ref[...] + _sigmoid(gate) * proj


def _ple_call(h, g, p, w_gate, w_proj, tm, tn):
    m, d = h.shape
    pd = p.shape[1]
    return pl.pallas_call(
        _ple_kernel,
        grid=(m // tm, d // tn),
        in_specs=[
            pl.BlockSpec((tm, d), lambda i, j: (i, 0)),
            pl.BlockSpec((tm, tn), lambda i, j: (i, j)),
            pl.BlockSpec((1, d), lambda i, j: (0, 0)),
            pl.BlockSpec((tm, pd), lambda i, j: (i, 0)),
            pl.BlockSpec((d, tn), lambda i, j: (0, j)),
            pl.BlockSpec((pd, tn), lambda i, j: (0, j)),
        ],
        out_specs=pl.BlockSpec((tm, tn), lambda i, j: (i, j)),
        out_shape=jax.ShapeDtypeStruct((m, d), F32),
        scratch_shapes=[pltpu.VMEM((tm, d), BF16), pltpu.VMEM((tm, pd), BF16)],
        compiler_params=_params(2),
        name="ple",
    )(h, h, g, p, w_gate, w_proj)


def _qkv_kernel(h_ref, gkv_ref, gmix_ref, wk_ref, wv_ref, wq_ref, gk_ref, gq_ref,
                wf_ref, bf_ref, tri_ref, place_ref, *refs, tm, tn, tps, transposed):
    if transposed:
        (k_ref, v_ref, kb_ref, kx_ref, vt_ref, qt_ref, lf_ref,
         xkv_ref, xq_ref, carry_ref, cp_ref) = refs
    else:
        (k_ref, v_ref, kb_ref, vb_ref, q_ref, lf_ref, c_ref,
         xkv_ref, xq_ref, carry_ref) = refs
    i = pl.program_id(0)
    j = pl.program_id(1)

    @pl.when(j == 0)
    def _():
        r = _rms_rows(h_ref[...])
        xkv = (r * gkv_ref[...]).astype(BF16)
        xkv_ref[...] = xkv
        xq_ref[...] = (r * gmix_ref[...]).astype(BF16)
        lf = _log_sigmoid(_dot(xkv, wf_ref[...]) + bf_ref[...])
        lf_ref[...] = lf

        @pl.when((i % tps) == 0)
        def _():
            carry_ref[...] = jnp.zeros(carry_ref.shape, F32)

        c = _dot3_left(tri_ref[...], lf) + carry_ref[...]
        carry_ref[...] = c[tm - 1:tm, :]
        if transposed:
            for piece, val in enumerate(_split3(-LOG2_E * c)):
                cp_ref[piece] = val
        else:
            c_ref[...] = c

    xkv = xkv_ref[...]
    kk = _dot(xkv, wk_ref[...])
    vv = _dot(xkv, wv_ref[...])
    qq = _dot(xq_ref[...], wq_ref[...])
    v_ref[...] = vv
    scale = HEAD_DIM ** -0.5 * (LOG2_E if transposed else 1.0)
    qn_heads = []
    for c0 in range(0, tn, HEAD_DIM):
        sl = slice(c0, c0 + HEAD_DIM)
        kn = _rms_rows(kk[:, sl]) * gk_ref[...]
        k_ref[:, sl] = kn
        kb_ref[:, sl] = kn.astype(BF16)
        qn_heads.append(_rms_rows(qq[:, sl]) * gq_ref[...] * scale)
    qn = jnp.concatenate(qn_heads, axis=1)
    if transposed:
        vt_ref[0] = vv.T.astype(BF16)
        qt_ref[0] = qn.T.astype(BF16)
        kx = _dot(cp_ref[0], place_ref[0])
        for piece in range(1, N_SPLIT):
            kx = kx + _dot(cp_ref[piece], place_ref[piece])
        kx_ref[...] = kx.astype(BF16)
    else:
        vb_ref[...] = vv.astype(BF16)
        q_ref[...] = qn.astype(BF16)


def _qkv_call(h, g_kv, g_mix, w_k, w_v, w_q, g_k, g_q, w_f_pad, b_f_pad,
              tri, place, tm, tn, seq_len, transposed):
    m, d = h.shape
    tps = max(seq_len // tm, 1)
    kern = functools.partial(_qkv_kernel, tm=tm, tn=tn, tps=tps, transposed=transposed)
    row = lambda i, j: (i, 0)
    const = lambda i, j: (0, 0)
    col = lambda i, j: (0, j)
    tile = lambda i, j: (i, j)
    tile_t = lambda i, j: (i, j, 0)
    in_specs = [
        pl.BlockSpec((tm, d), row),
        pl.BlockSpec((1, d), const),
        pl.BlockSpec((1, d), const),
        pl.BlockSpec((d, tn), col),
        pl.BlockSpec((d, tn), col),
        pl.BlockSpec((d, tn), col),
        pl.BlockSpec((1, HEAD_DIM), const),
        pl.BlockSpec((1, HEAD_DIM), const),
        pl.BlockSpec((d, LANES), const),
        pl.BlockSpec((1, LANES), const),
        pl.BlockSpec((tm, tm), const),
        pl.BlockSpec((N_SPLIT, LANES, tn), lambda i, j: (0, 0, j)),
    ]
    scratch = [pltpu.VMEM((tm, d), BF16), pltpu.VMEM((tm, d), BF16),
               pltpu.VMEM((1, LANES), F32)]
    f32_md = jax.ShapeDtypeStruct((m, d), F32)
    bf_md = jax.ShapeDtypeStruct((m, d), BF16)
    if transposed:
        out_specs = [
            pl.BlockSpec((tm, tn), tile), pl.BlockSpec((tm, tn), tile),
            pl.BlockSpec((tm, tn), tile), pl.BlockSpec((tm, tn), tile),
            pl.BlockSpec((1, tn, tm), tile_t), pl.BlockSpec((1, tn, tm), tile_t),
            pl.BlockSpec((tm, LANES), row),
        ]
        bf_t = jax.ShapeDtypeStruct((m // tm, d, tm), BF16)
        out_shape = [f32_md, f32_md, bf_md, bf_md, bf_t, bf_t,
                     jax.ShapeDtypeStruct((m, LANES), F32)]
        scratch.append(pltpu.VMEM((N_SPLIT, tm, LANES), BF16))
    else:
        out_specs = [
            pl.BlockSpec((tm, tn), tile), pl.BlockSpec((tm, tn), tile),
            pl.BlockSpec((tm, tn), tile), pl.BlockSpec((tm, tn), tile),
            pl.BlockSpec((tm, tn), tile),
            pl.BlockSpec((tm, LANES), row), pl.BlockSpec((tm, LANES), row),
        ]
        out_shape = [f32_md, f32_md, bf_md, bf_md, bf_md,
                     jax.ShapeDtypeStruct((m, LANES), F32),
                     jax.ShapeDtypeStruct((m, LANES), F32)]
    return pl.pallas_call(
        kern,
        grid=(m // tm, d // tn),
        in_specs=in_specs,
        out_specs=out_specs,
        out_shape=out_shape,
        scratch_shapes=scratch,
        compiler_params=_params(2),
        name="qkv_t" if transposed else "qkv",
    )(h, g_kv, g_mix, w_k, w_v, w_q, g_k, g_q, w_f_pad, b_f_pad, tri, place)


def _attn_kernel(q_ref, k_ref, kx_ref, v_ref, o_ref,
                 s0_ref, s1_ref, bm0_ref, bm1_ref, m_ref, l_ref, acc_ref, *, tq):
    qi = pl.program_id(2)
    ones_rows = (lax.broadcasted_iota(jnp.int32, (HEAD_DIM, tq), 0) < N_SPLIT)
    q_aug = jnp.concatenate([q_ref[0], ones_rows.astype(BF16)], axis=0)
    s_refs = (s0_ref, s1_ref)
    bm_refs = (bm0_ref, bm1_ref)
    m_ref[...] = jnp.full(m_ref.shape, NEG_INF, F32)
    l_ref[...] = jnp.zeros(l_ref.shape, F32)
    acc_ref[...] = jnp.zeros(acc_ref.shape, F32)

    def scores(kb, slot, diagonal):
        start = pl.multiple_of(kb * tq, tq)
        k_aug = jnp.concatenate([k_ref[pl.ds(start, tq), :],
                                 kx_ref[pl.ds(start, tq), :]], axis=1)
        s = _dot(k_aug, q_aug)
        if diagonal:
            key = lax.broadcasted_iota(jnp.int32, (tq, tq), 0)
            qry = lax.broadcasted_iota(jnp.int32, (tq, tq), 1)
            s = jnp.where(key <= qry, s, NEG_INF)
        s_refs[slot][...] = s
        bm_refs[slot][...] = jnp.max(s, axis=0, keepdims=True)

    def accumulate(kb, slot):
        m_old = m_ref[...]
        m_new = jnp.maximum(m_old, bm_refs[slot][...])
        alpha = jnp.exp2(m_old - m_new)
        p = jnp.exp2(s_refs[slot][...] - m_new)
        l_ref[...] = alpha * l_ref[...] + jnp.sum(p, axis=0, keepdims=True)
        acc_ref[...] = alpha * acc_ref[...] + _dot(v_ref[kb], p.astype(BF16))
        m_ref[...] = m_new

    def by_parity(kb, fn):
        for parity in (0, 1):
            @pl.when(kb % 2 == parity)
            def _():
                fn(parity)

    @pl.when(qi == 0)
    def _():
        scores(0, 0, True)

    @pl.when(qi > 0)
    def _():
        scores(0, 0, False)

    def steady(kb, carry):
        def fn(parity):
            scores(kb + 1, 1 - parity, False)
            accumulate(kb, parity)
        by_parity(kb, fn)
        return carry

    lax.fori_loop(0, qi - 1, steady, 0)

    @pl.when(qi > 0)
    def _():
        def fn(parity):
            scores(qi, 1 - parity, True)
            accumulate(qi - 1, parity)
        by_parity(qi - 1, fn)

    by_parity(qi, lambda parity: accumulate(qi, parity))
    o_ref[...] = (acc_ref[...] / l_ref[...]).T.astype(o_ref.dtype)


def _attn_call(q_t, k, kx, v_t, n_seq, seq_len, tq):
    d = k.shape[1]
    nh = d // HEAD_DIM
    nb = seq_len // tq
    kern = functools.partial(_attn_kernel, tq=tq)
    return pl.pallas_call(
        kern,
        grid=(n_seq, nh, nb),
        in_specs=[
            pl.BlockSpec((1, HEAD_DIM, tq), lambda n, h, i: (n * nb + i, h, 0)),
            pl.BlockSpec((seq_len, HEAD_DIM), lambda n, h, i: (n, h)),
            pl.BlockSpec((seq_len, HEAD_DIM), lambda n, h, i: (n, h)),
            pl.BlockSpec((nb, HEAD_DIM, tq), lambda n, h, i: (n, h, 0)),
        ],
        out_specs=pl.BlockSpec((tq, HEAD_DIM), lambda n, h, i: (n * nb + i, h)),
        out_shape=jax.ShapeDtypeStruct((n_seq * seq_len, d), BF16),
        scratch_shapes=[pltpu.VMEM((tq, tq), F32), pltpu.VMEM((tq, tq), F32),
                        pltpu.VMEM((1, tq), F32), pltpu.VMEM((1, tq), F32),
                        pltpu.VMEM((1, tq), F32), pltpu.VMEM((1, tq), F32),
                        pltpu.VMEM((HEAD_DIM, tq), F32)],
        compiler_params=_params(3),
        name="fox_prompt",
    )(q_t, k, kx, v_t)


def _bias_kernel(pt_ref, *refs, n_steps):
    nb = BIAS_PAGES_PER_STEP
    lf_refs = refs[0:nb]
    sfx_ref, tot_ref, later_ref, spread_ref, o_ref, x_ref = refs[nb:]
    s = pl.program_id(1)
    base = pl.multiple_of(s * nb, nb)
    for e in range(nb):
        x_ref[pl.ds(base + e, 1), :] = lf_refs[e][0]

    @pl.when(s == n_steps - 1)
    def _():
        x = x_ref[...]
        within = _dot3(x, sfx_ref[...])
        tot = _dot3(x, tot_ref[...])
        later = _dot3_left(later_ref[...], tot)
        o_ref[0] = within + _dot3(later, spread_ref[...])


def _bias_call(page_table, lf_flat, sfx, tot_m, later_m, spread):
    b, n_pages = page_table.shape
    width = lf_flat.shape[2]
    nb = BIAS_PAGES_PER_STEP
    n_steps = n_pages // nb

    def page_map(e):
        return lambda bi, s, pt: (pt[bi, s * nb + e], 0, 0)

    const2 = lambda bi, s, pt: (0, 0)
    in_specs = [pl.BlockSpec((1, 1, width), page_map(e)) for e in range(nb)]
    in_specs += [pl.BlockSpec(sfx.shape, const2), pl.BlockSpec(tot_m.shape, const2),
                 pl.BlockSpec(later_m.shape, const2), pl.BlockSpec(spread.shape, const2)]
    grid_spec = pltpu.PrefetchScalarGridSpec(
        num_scalar_prefetch=1,
        grid=(b, n_steps),
        in_specs=in_specs,
        out_specs=pl.BlockSpec((1, n_pages, width), lambda bi, s, pt: (bi, 0, 0)),
        scratch_shapes=[pltpu.VMEM((n_pages, width), F32)],
    )
    return pl.pallas_call(
        functools.partial(_bias_kernel, n_steps=n_steps),
        grid_spec=grid_spec,
        out_shape=jax.ShapeDtypeStruct((b, n_pages, width), F32),
        compiler_params=_params(2),
        name="decode_bias",
    )(page_table, *([lf_flat] * nb), sfx, tot_m, later_m, spread)


def _sattn_kernel(pt_ref, q_ref, *refs, n_steps, n_grp):
    g_pages = PAGES_PER_STEP
    k_refs = refs[0:g_pages]
    v_refs = refs[g_pages:2 * g_pages]
    (bias_ref, mask_ref, kn_ref, vn_ref, bn_ref, maskn_ref, o_ref,
     m_ref, l_ref, acc_ref) = refs[2 * g_pages:]
    g = pl.program_id(1)
    keys_g = mask_ref.shape[1]
    dh = q_ref.shape[3]

    @pl.when(g == 0)
    def _():
        m_ref[...] = jnp.full(m_ref.shape, NEG_INF, F32)
        l_ref[...] = jnp.zeros(l_ref.shape, F32)
        acc_ref[...] = jnp.zeros(acc_ref.shape, F32)

    def update(gi, s, v_rows):
        m_old = m_ref[gi]
        m_new = jnp.maximum(m_old, jnp.max(s, axis=-1, keepdims=True))
        alpha = jnp.exp(m_old - m_new)
        p = jnp.exp(s - m_new)
        l_ref[gi] = alpha * l_ref[gi] + jnp.sum(p, axis=-1, keepdims=True)
        acc_ref[gi] = alpha * acc_ref[gi] + _dot(p.astype(BF16), v_rows)
        m_ref[gi] = m_new

    def group_rows(page_ref, gi):
        tiles = page_ref[0, pl.ds(gi, PAGE_SIZE, stride=n_grp), :, :]
        return tiles.reshape(keys_g, dh).astype(BF16)

    steps_per_block = SUBLANES // g_pages
    bias_rows = [bias_ref[0, pl.ds((g % steps_per_block) * g_pages + e, 1), :]
                 for e in range(g_pages)]

    def group_scores(gi):
        parts = [_dot_nt(q_ref[0, gi], group_rows(k_refs[e], gi))
                 + (mask_ref[...] + bias_rows[e][:, gi * keys_g:(gi + 1) * keys_g])
                 for e in range(g_pages)]
        return jnp.concatenate(parts, axis=1)

    scores = [group_scores(gi) for gi in range(n_grp)]
    for gi in range(n_grp):
        values = jnp.concatenate([group_rows(v_refs[e], gi) for e in range(g_pages)], axis=0)
        update(gi, scores[gi], values)

    @pl.when(g == n_steps - 1)
    def _():
        for gi in range(n_grp):
            s = _dot_nt(q_ref[0, gi], kn_ref[0, gi]) + (maskn_ref[...] + bn_ref[0, gi])
            update(gi, s, vn_ref[0, gi])
        o_ref[0] = (acc_ref[...] / l_ref[...]).astype(o_ref.dtype)


def _sattn_call(page_table, q4, cache_k, cache_v, bias, mask, k_new, v_new,
                bias_new, mask_new):
    b, n_grp, rows, dh = q4.shape
    n_pages = page_table.shape[1]
    n_steps = n_pages // PAGES_PER_STEP
    page_tiles = cache_k.shape[1]
    width = bias.shape[2]
    steps_per_block = SUBLANES // PAGES_PER_STEP

    def page_map(e):
        return lambda bi, g, pt: (pt[bi, g * PAGES_PER_STEP + e], 0, 0, 0)

    per_seq = lambda bi, g, pt: (bi, 0, 0, 0)
    const2 = lambda bi, g, pt: (0, 0)
    page_spec = lambda e: pl.BlockSpec((1, page_tiles, SUBLANES, dh), page_map(e))
    in_specs = [pl.BlockSpec((1, n_grp, rows, dh), per_seq)]
    in_specs += [page_spec(e) for e in range(PAGES_PER_STEP)]
    in_specs += [page_spec(e) for e in range(PAGES_PER_STEP)]
    in_specs += [
        pl.BlockSpec((1, SUBLANES, width), lambda bi, g, pt: (bi, g // steps_per_block, 0)),
        pl.BlockSpec(mask.shape, const2),
        pl.BlockSpec((1, n_grp, rows, dh), per_seq),
        pl.BlockSpec((1, n_grp, rows, dh), per_seq),
        pl.BlockSpec((1, n_grp, 1, rows), per_seq),
        pl.BlockSpec(mask_new.shape, const2),
    ]
    grid_spec = pltpu.PrefetchScalarGridSpec(
        num_scalar_prefetch=1,
        grid=(b, n_steps),
        in_specs=in_specs,
        out_specs=pl.BlockSpec((1, n_grp, rows, dh), per_seq),
        scratch_shapes=[pltpu.VMEM((n_grp, rows, 1), F32), pltpu.VMEM((n_grp, rows, 1), F32),
                        pltpu.VMEM((n_grp, rows, dh), F32)],
    )
    kern = functools.partial(_sattn_kernel, n_steps=n_steps, n_grp=n_grp)
    return pl.pallas_call(
        kern,
        grid_spec=grid_spec,
        out_shape=jax.ShapeDtypeStruct((b, n_grp, rows, dh), BF16),
        compiler_params=_params(2),
        name="fox_decode",
    )(page_table, q4, *([cache_k] * PAGES_PER_STEP), *([cache_v] * PAGES_PER_STEP),
      bias, mask, k_new, v_new, bias_new, mask_new)


def _oproj_kernel(a_ref, w_ref, h_ref, o_ref):
    o_ref[...] = h_ref[...] + _dot(a_ref[...], w_ref[...])


def _oproj_call(a, w_o, h, tm, tn):
    m, d = h.shape
    return pl.pallas_call(
        _oproj_kernel,
        grid=(m // tm, d // tn),
        in_specs=[
            pl.BlockSpec((tm, d), lambda i, j: (i, 0)),
            pl.BlockSpec((d, tn), lambda i, j: (0, j)),
            pl.BlockSpec((tm, tn), lambda i, j: (i, j)),
        ],
        out_specs=pl.BlockSpec((tm, tn), lambda i, j: (i, j)),
        out_shape=jax.ShapeDtypeStruct((m, d), F32),
        compiler_params=_params(2),
        name="o_proj",
    )(a, w_o, h)


def _cumsum_matrix(n, group):
    t = jnp.arange(n)[:, None]
    tp = jnp.arange(n)[None, :]
    return ((tp <= t) & (tp // group == t // group)).astype(BF16)


def _placement(nh, d):
    lane = jnp.arange(LANES)[None, :, None]
    col = jnp.arange(d)[None, None, :]
    piece = jnp.arange(N_SPLIT)[:, None, None]
    return ((lane < nh) & (col == lane * HEAD_DIM + piece)).astype(BF16)


def kernel(x_prompt, x_sample, state_conv_a, state_ffn, cache_k, cache_v, cache_logf, page_table, p_prompt, p_sample, g_mix_a, w_pw1, w_dwa, b_dwa, ln_g, ln_b, w_pw2, g_kv, w_k, w_v, g_k, w_f, b_f, g_mix_b, w_q, g_q, w_o, g_ffn, w_up, w_gate, w_fdw, b_fdw, w_down, g_ple, w_ple_gate, w_ple_proj):
    n_p, t_p, d = x_prompt.shape
    n_s, t_s, _ = x_sample.shape
    depth = g_ffn.shape[0]
    n_a = g_mix_a.shape[0]
    f = w_up.shape[2]
    nh = d // HEAD_DIM
    m_s = n_s * t_s

    bf = lambda w: w.astype(BF16)
    w_pw1_b, w_pw2_b = bf(w_pw1), bf(w_pw2)
    w_k_b, w_v_b, w_q_b, w_o_b = bf(w_k), bf(w_v), bf(w_q), bf(w_o)
    w_up_b, w_gate_b, w_down_b = bf(w_up), bf(w_gate), bf(w_down)
    w_pg_b, w_pp_b = bf(w_ple_gate), bf(w_ple_proj)
    w_f_pad = jnp.pad(bf(w_f), ((0, 0), (0, LANES - nh)))
    b_f_pad = jnp.pad(b_f, (0, LANES - nh)).reshape(1, LANES)
    place = _placement(nh, d)
    row = lambda a: a.reshape(1, -1)

    def trunk(x, p, n_seq, seq_len, a_prev32, tiles, ffn, attend, tri, transposed):
        h = x.reshape(n_seq * seq_len, d)
        a_us, f_bufs = [], []
        k = v = lf = None
        for i in range(depth):
            if i < n_a:
                u = _glu_call(h, row(g_mix_a[i]), w_pw1_b[i], tiles.rows, tiles.cols)
                h = _conva_call(u, a_prev32[i], w_dwa[i], row(b_dwa[i]), row(ln_g[i]),
                                row(ln_b[i]), w_pw2_b[i], h, tiles.rows_conv, seq_len)
                a_us.append(u.reshape(n_seq, seq_len, d))
            else:
                jb = i - n_a
                outs = _qkv_call(
                    h, row(g_kv), row(g_mix_b[jb]), w_k_b, w_v_b, w_q_b[jb],
                    row(g_k), row(g_q[jb]), w_f_pad, b_f_pad, tri, place,
                    tiles.rows_qkv, tiles.cols, seq_len, transposed)
                k, v = outs[0], outs[1]
                lf = outs[6] if transposed else outs[5]
                a = attend(outs)
                h = _oproj_call(a, w_o_b[jb], h, tiles.rows, tiles.cols_oproj)
            h, fb = ffn(i, h)
            f_bufs.append(fb)
            h = _ple_call(h, row(g_ple[i]), p[i].reshape(n_seq * seq_len, -1),
                          w_pg_b[i], w_pp_b[i], tiles.rows, tiles.cols)
        shp = (n_seq, seq_len, nh, HEAD_DIM)
        return (h.reshape(n_seq, seq_len, d), a_us, jnp.stack(f_bufs),
                k.reshape(shp), v.reshape(shp),
                lf[:, :nh].reshape(n_seq, seq_len, nh))

    tp = PROMPT_TILES

    def ffn_prompt(i, h):
        prev = jnp.zeros((n_p, CONV_F_WIDTH - 1, f), F32)
        return _ffn_seq_call(h, row(g_ffn[i]), w_up_b[i], w_gate_b[i], w_fdw[i],
                             row(b_fdw[i]), w_down_b[i], prev, tp.rows_ffn, tp.cols, t_p)

    def attend_prompt(outs):
        _, _, kb, kx, v_t, q_t, _ = outs
        return _attn_call(q_t, kb, kx, v_t, n_p, t_p, tp.rows_qkv)

    a_prev_p = [jnp.zeros((n_p, HALO_ROWS, d), F32) for _ in range(n_a)]
    (y_p, a_us_p, ffn_p, k_p, v_p, lf_p) = trunk(
        x_prompt, p_prompt, n_p, t_p, a_prev_p, tp, ffn_prompt,
        attend_prompt, _cumsum_matrix(tp.rows_qkv, tp.rows_qkv), True)
    conv_a_p = jnp.stack([u[:, t_p - (CONV_A_WIDTH - 1):] for u in a_us_p])

    def ffn_sample(i, h):
        st = state_ffn[i]
        z = jnp.zeros((n_s, t_s, f), F32)
        p1 = z.at[:, 0].set(st[:, 1]).reshape(m_s, f)
        p2 = z.at[:, 0].set(st[:, 0]).at[:, 1].set(st[:, 1]).reshape(m_s, f)
        h, u = _ffn_multi_call(h, row(g_ffn[i]), w_up_b[i], w_gate_b[i], w_fdw[i],
                               row(b_fdw[i]), w_down_b[i], p1, p2, tp.cols, t_s)
        full = jnp.concatenate([st, u.reshape(n_s, t_s, f)], axis=1)
        return h, full[:, -(CONV_F_WIDTH - 1):]

    n_pool = cache_k.shape[0]
    n_pages = page_table.shape[1]
    n_grp = nh // SUBLANES
    keys_g = PAGE_SIZE * SUBLANES
    page_rows = PAGE_SIZE * nh
    tiles_shape = (n_pool, PAGE_SIZE * n_grp, SUBLANES, HEAD_DIM)
    cache_k4 = cache_k.reshape(tiles_shape)
    cache_v4 = cache_v.reshape(tiles_shape)
    lf_flat = cache_logf.reshape(n_pool, 1, page_rows)
    fr = jnp.arange(page_rows)
    t_in, h_in = fr // nh, fr % nh
    t_out = (fr % keys_g) // SUBLANES
    h_out = (fr // keys_g) * SUBLANES + fr % SUBLANES
    lane = jnp.arange(LANES)
    sfx = ((h_in[:, None] == h_out[None, :]) & (t_in[:, None] > t_out[None, :])).astype(BF16)
    tot_m = (h_in[:, None] == lane[None, :]).astype(BF16)
    pg = jnp.arange(n_pages)
    later_m = (pg[None, :] > pg[:, None]).astype(BF16)
    spread = (lane[:, None] == h_out[None, :]).astype(BF16)
    qrow = jnp.arange(t_s * SUBLANES)
    kcol = jnp.arange(keys_g)
    mask = jnp.where((qrow[:, None] % SUBLANES) == (kcol[None, :] % SUBLANES),
                     0.0, NEG_INF).astype(F32)
    mask_new = jnp.where(((qrow[:, None] % SUBLANES) == (qrow[None, :] % SUBLANES))
                         & (qrow[None, :] // SUBLANES <= qrow[:, None] // SUBLANES),
                         0.0, NEG_INF).astype(F32)
    bias = _bias_call(page_table, lf_flat, sfx, tot_m, later_m, spread)

    def by_group(a, width):
        a5 = a.reshape(n_s, t_s, n_grp, SUBLANES, width)
        return jnp.transpose(a5, (0, 2, 1, 3, 4)).reshape(n_s, n_grp, t_s * SUBLANES, width)

    def attend_sample(outs):
        _, _, kb, vb, q, _, c = outs
        bias_new = -by_group(c[:, :nh], 1).reshape(n_s, n_grp, 1, t_s * SUBLANES)
        a = _sattn_call(page_table, by_group(q, HEAD_DIM), cache_k4, cache_v4, bias, mask,
                        by_group(kb, HEAD_DIM), by_group(vb, HEAD_DIM), bias_new, mask_new)
        a5 = a.reshape(n_s, n_grp, t_s, SUBLANES, HEAD_DIM)
        return jnp.transpose(a5, (0, 2, 1, 3, 4)).reshape(m_s, d)

    a_prev_s = [jnp.pad(state_conv_a[i], ((0, 0), (HALO_ROWS - (CONV_A_WIDTH - 1), 0), (0, 0)))
                for i in range(n_a)]
    ts = _Tiles(rows=m_s, rows_qkv=m_s, rows_ffn=m_s, rows_conv=t_s,
                cols=tp.cols, cols_oproj=tp.cols)
    (y_s, a_us_s, ffn_s, k_s, v_s, lf_s) = trunk(
        x_sample, p_sample, n_s, t_s, a_prev_s, ts, ffn_sample,
        attend_sample, _cumsum_matrix(m_s, t_s), False)
    conv_a_s = jnp.stack([
        jnp.concatenate([state_conv_a[i], a_us_s[i]], axis=1)[:, -(CONV_A_WIDTH - 1):]
        for i in range(n_a)])

    return (y_p, y_s, conv_a_p, ffn_p, k_p, v_p, lf_p,
            conv_a_s, ffn_s, k_s, v_s, lf_s)
```

```python
import functools
from typing import NamedTuple

import jax
import jax.numpy as jnp
import numpy as np
from jax import lax
from jax.experimental import pallas as pl
from jax.experimental.pallas import tpu as pltpu

F32 = jnp.float32
BF16 = jnp.bfloat16

HEAD_DIM = 128
CONV_A_WIDTH = 31
CONV_F_WIDTH = 3
PAGE_SIZE = 128
EPS = 1e-6
NEG_INF = -1e30
LOG2_E = 1.4426950408889634

LANES = 128
SUBLANES = 8
HALO_ROWS = 32
VMEM_LIMIT = 56 * 1024 * 1024
PAGES_PER_STEP = 8
BIAS_PAGES_PER_STEP = 16
N_SPLIT = 3


class _Tiles(NamedTuple):
    rows: int
    rows_qkv: int
    rows_ffn: int
    rows_conv: int
    rows_ple: int
    cols: int
    cols_oproj: int


PROMPT_TILES = _Tiles(rows=1024, rows_qkv=512, rows_ffn=512, rows_conv=256,
                      rows_ple=512, cols=512, cols_oproj=1024)


def _params(n_axes):
    return pltpu.CompilerParams(
        dimension_semantics=("arbitrary",) * n_axes,
        vmem_limit_bytes=VMEM_LIMIT)


def _dot(a, b):
    return jnp.dot(a, b, preferred_element_type=F32)


def _dot_nt(a, b):
    return lax.dot_general(a, b, (((1,), (1,)), ((), ())),
                           preferred_element_type=F32)


def _sigmoid(x):
    return jax.nn.sigmoid(x)


def _log_sigmoid(z):
    return -(jnp.maximum(-z, 0.0) + jnp.log1p(jnp.exp(-jnp.abs(z))))


def _rms_rows(x):
    ms = jnp.mean(x * x, axis=-1, keepdims=True)
    return x * lax.rsqrt(ms + EPS)


def _split3(x):
    hi = x.astype(BF16)
    r1 = x - hi.astype(F32)
    mid = r1.astype(BF16)
    lo = (r1 - mid.astype(F32)).astype(BF16)
    return hi, mid, lo


def _dot3(x, w_bf16):
    hi, mid, lo = _split3(x)
    return _dot(hi, w_bf16) + _dot(mid, w_bf16) + _dot(lo, w_bf16)


def _dot3_left(w_bf16, x):
    hi, mid, lo = _split3(x)
    return _dot(w_bf16, hi) + _dot(w_bf16, mid) + _dot(w_bf16, lo)


def _glu_kernel(x_ref, g_ref, w1_ref, w2_ref, u_ref, xn_ref):
    @pl.when(pl.program_id(1) == 0)
    def _():
        xn_ref[...] = (_rms_rows(x_ref[...]) * g_ref[...]).astype(BF16)

    xn = xn_ref[...]
    a1 = _dot(xn, w1_ref[...])
    a2 = _dot(xn, w2_ref[...])
    u_ref[...] = a1 * _sigmoid(a2)


def _glu_call(h, g, w_pw1, tm, tn):
    m, d = h.shape
    nj = d // tn
    return pl.pallas_call(
        _glu_kernel,
        grid=(m // tm, nj),
        in_specs=[
            pl.BlockSpec((tm, d), lambda i, j: (i, 0)),
            pl.BlockSpec((1, d), lambda i, j: (0, 0)),
            pl.BlockSpec((d, tn), lambda i, j: (0, j)),
            pl.BlockSpec((d, tn), lambda i, j: (0, j + nj)),
        ],
        out_specs=pl.BlockSpec((tm, tn), lambda i, j: (i, j)),
        out_shape=jax.ShapeDtypeStruct((m, d), F32),
        scratch_shapes=[pltpu.VMEM((tm, d), BF16)],
        compiler_params=_params(2),
        name="glu",
    )(h, g, w_pw1, w_pw1)


def _conva_kernel(u_ref, halo_ref, prev_ref, wdw_ref, bdw_ref, lng_ref, lnb_ref,
                  w2_ref, h_ref, o_ref, full_ref, c_ref, *, tm, tps, rb):
    i = pl.program_id(0)
    d = u_ref.shape[1]
    first = (i % tps) == 0
    full_ref[0:HALO_ROWS, :] = jnp.where(first, prev_ref[0], halo_ref[...])
    full_ref[HALO_ROWS:HALO_ROWS + tm, :] = u_ref[...]
    full_ref[HALO_ROWS + tm:HALO_ROWS + tm + SUBLANES, :] = jnp.zeros((SUBLANES, d), F32)
    off = HALO_ROWS - (CONV_A_WIDTH - 1)

    def conv_chunk(r0, c0):
        cols = slice(c0, c0 + LANES)
        acc = jnp.broadcast_to(bdw_ref[:, cols], (rb, LANES))
        for b in range(SUBLANES):
            z = None
            for j in range(CONV_A_WIDTH):
                if (off + j) % SUBLANES != b:
                    continue
                s = r0 + (off + j) // SUBLANES * SUBLANES
                term = wdw_ref[j:j + 1, cols] * full_ref[s:s + rb + SUBLANES, cols]
                z = term if z is None else z + term
            acc = acc + z[b:b + rb, :]
        c_ref[r0:r0 + rb, cols] = acc

    for r0 in range(0, tm, rb):
        for c0 in range(0, d, LANES):
            conv_chunk(r0, c0)
    c = c_ref[...]
    mu = jnp.mean(c, axis=-1, keepdims=True)
    xc = c - mu
    var = jnp.mean(xc * xc, axis=-1, keepdims=True)
    y = xc * lax.rsqrt(var + EPS) * lng_ref[...] + lnb_ref[...]
    a = (y * _sigmoid(y)).astype(BF16)
    o_ref[...] = h_ref[...] + _dot(a, w2_ref[...])


def _conva_call(u, prev32, w_dw, b_dw, ln_g, ln_b, w_pw2, h, tm, seq_len):
    m, d = u.shape
    tps = max(seq_len // tm, 1)
    rb = min(tm, 128)
    if tm % HALO_ROWS == 0:
        k = tm // HALO_ROWS
        halo_map = lambda i: (jnp.maximum(i * k - 1, 0), 0)
    else:
        assert tps == 1
        halo_map = lambda i: (0, 0)
    kern = functools.partial(_conva_kernel, tm=tm, tps=tps, rb=rb)
    return pl.pallas_call(
        kern,
        grid=(m // tm,),
        in_specs=[
            pl.BlockSpec((tm, d), lambda i: (i, 0)),
            pl.BlockSpec((HALO_ROWS, d), halo_map),
            pl.BlockSpec((1, HALO_ROWS, d), lambda i: (i // tps, 0, 0)),
            pl.BlockSpec((CONV_A_WIDTH, d), lambda i: (0, 0)),
            pl.BlockSpec((1, d), lambda i: (0, 0)),
            pl.BlockSpec((1, d), lambda i: (0, 0)),
            pl.BlockSpec((1, d), lambda i: (0, 0)),
            pl.BlockSpec((d, d), lambda i: (0, 0), pipeline_mode=pl.Buffered(1)),
            pl.BlockSpec((tm, d), lambda i: (i, 0)),
        ],
        out_specs=pl.BlockSpec((tm, d), lambda i: (i, 0)),
        out_shape=jax.ShapeDtypeStruct((m, d), F32),
        scratch_shapes=[pltpu.VMEM((HALO_ROWS + tm + SUBLANES, d), F32),
                        pltpu.VMEM((tm, d), F32)],
        compiler_params=_params(1),
        name="conv_a",
    )(u, u, prev32, w_dw, b_dw, ln_g, ln_b, w_pw2, h)


def _ffn_body(h_ref, g_ref, wu_ref, wg_ref, wdw_ref, bdw_ref, wd_ref,
              o_ref, xn_ref, ubuf_ref, tm, prev_rows, fix_shifted):
    xn = xn_ref[...]
    u = _dot(xn, wu_ref[...])
    ubuf_ref[SUBLANES:SUBLANES + tm, :] = u
    ubuf_ref[SUBLANES - 2:SUBLANES, :] = prev_rows
    u1 = ubuf_ref[SUBLANES - 1:SUBLANES - 1 + tm, :]
    u2 = ubuf_ref[SUBLANES - 2:SUBLANES - 2 + tm, :]
    u1, u2 = fix_shifted(u1, u2)
    c = (wdw_ref[0:1, :] * u2 + wdw_ref[1:2, :] * u1 + wdw_ref[2:3, :] * u
         + bdw_ref[...])
    gt = _dot(xn, wg_ref[...])
    act = (c * _sigmoid(c) * gt).astype(BF16)
    o_ref[...] += _dot(act, wd_ref[...])


def _ffn_seq_kernel(h_ref, g_ref, wu_ref, wg_ref, wdw_ref, bdw_ref, wd_ref,
                    prev_ref, o_ref, fb_ref, xn_ref, ubuf_ref, carry_ref,
                    *, tm, tps):
    i = pl.program_id(0)
    j = pl.program_id(1)
    first = (i % tps) == 0

    @pl.when(j == 0)
    def _():
        xn_ref[...] = (_rms_rows(h_ref[...]) * g_ref[...]).astype(BF16)
        o_ref[...] = h_ref[...]

        @pl.when(i == 0)
        def _():
            carry_ref[...] = jnp.zeros(carry_ref.shape, F32)

    prev_rows = jnp.where(first, prev_ref[0], carry_ref[j])
    _ffn_body(h_ref, g_ref, wu_ref, wg_ref, wdw_ref, bdw_ref, wd_ref,
              o_ref, xn_ref, ubuf_ref, tm, prev_rows, lambda a, b: (a, b))
    tail = ubuf_ref[SUBLANES + tm - 2:SUBLANES + tm, :]
    carry_ref[j] = tail

    @pl.when((i % tps) == tps - 1)
    def _():
        fb_ref[i // tps, j] = tail


def _ffn_multi_kernel(h_ref, g_ref, wu_ref, wg_ref, wdw_ref, bdw_ref, wd_ref,
                      p1_ref, p2_ref, o_ref, u_ref, xn_ref, ubuf_ref,
                      *, tm, seq_len):
    @pl.when(pl.program_id(1) == 0)
    def _():
        xn_ref[...] = (_rms_rows(h_ref[...]) * g_ref[...]).astype(BF16)
        o_ref[...] = h_ref[...]
        ubuf_ref[0:SUBLANES, :] = jnp.zeros((SUBLANES, ubuf_ref.shape[1]), F32)

    def fix_shifted(u1, u2):
        t = lax.broadcasted_iota(jnp.int32, (tm, 1), 0) % seq_len
        return (jnp.where(t >= 1, u1, p1_ref[...]),
                jnp.where(t >= 2, u2, p2_ref[...]))

    prev_rows = jnp.zeros((CONV_F_WIDTH - 1, ubuf_ref.shape[1]), F32)
    _ffn_body(h_ref, g_ref, wu_ref, wg_ref, wdw_ref, bdw_ref, wd_ref,
              o_ref, xn_ref, ubuf_ref, tm, prev_rows, fix_shifted)
    u_ref[...] = ubuf_ref[SUBLANES:SUBLANES + tm, :]


def _ffn_common_specs(tm, tn, d, f):
    return [
        pl.BlockSpec((tm, d), lambda i, j: (i, 0)),
        pl.BlockSpec((1, d), lambda i, j: (0, 0)),
        pl.BlockSpec((d, tn), lambda i, j: (0, j)),
        pl.BlockSpec((d, tn), lambda i, j: (0, j)),
        pl.BlockSpec((CONV_F_WIDTH, tn), lambda i, j: (0, j)),
        pl.BlockSpec((1, tn), lambda i, j: (0, j)),
        pl.BlockSpec((tn, d), lambda i, j: (j, 0)),
    ]


def _ffn_seq_call(h, g, w_up, w_gate, w_dw, b_dw, w_down, prev, tm, tn, seq_len):
    m, d = h.shape
    f = w_up.shape[1]
    n_seq = m // seq_len
    tps = seq_len // tm
    nj = f // tn
    kern = functools.partial(_ffn_seq_kernel, tm=tm, tps=tps)
    h_new, fb = pl.pallas_call(
        kern,
        grid=(m // tm, nj),
        in_specs=_ffn_common_specs(tm, tn, d, f) + [
            pl.BlockSpec((1, CONV_F_WIDTH - 1, tn), lambda i, j: (i // tps, 0, j)),
        ],
        out_specs=[
            pl.BlockSpec((tm, d), lambda i, j: (i, 0)),
            pl.BlockSpec((n_seq, nj, CONV_F_WIDTH - 1, tn), lambda i, j: (0, 0, 0, 0)),
        ],
        out_shape=[jax.ShapeDtypeStruct((m, d), F32),
                   jax.ShapeDtypeStruct((n_seq, nj, CONV_F_WIDTH - 1, tn), F32)],
        scratch_shapes=[pltpu.VMEM((tm, d), BF16),
                        pltpu.VMEM((SUBLANES + tm, tn), F32),
                        pltpu.VMEM((nj, CONV_F_WIDTH - 1, tn), F32)],
        compiler_params=_params(2),
        name="ffn_seq",
    )(h, g, w_up, w_gate, w_dw, b_dw, w_down, prev)
    fb = jnp.transpose(fb, (0, 2, 1, 3)).reshape(n_seq, CONV_F_WIDTH - 1, f)
    return h_new, fb


def _ffn_multi_call(h, g, w_up, w_gate, w_dw, b_dw, w_down, p1, p2, tn, seq_len):
    m, d = h.shape
    f = w_up.shape[1]
    kern = functools.partial(_ffn_multi_kernel, tm=m, seq_len=seq_len)
    return pl.pallas_call(
        kern,
        grid=(1, f // tn),
        in_specs=_ffn_common_specs(m, tn, d, f) + [
            pl.BlockSpec((m, tn), lambda i, j: (0, j)),
            pl.BlockSpec((m, tn), lambda i, j: (0, j)),
        ],
        out_specs=[
            pl.BlockSpec((m, d), lambda i, j: (0, 0)),
            pl.BlockSpec((m, tn), lambda i, j: (0, j)),
        ],
        out_shape=[jax.ShapeDtypeStruct((m, d), F32),
                   jax.ShapeDtypeStruct((m, f), F32)],
        scratch_shapes=[pltpu.VMEM((m, d), BF16),
                        pltpu.VMEM((SUBLANES + m, tn), F32)],
        compiler_params=_params(2),
        name="ffn_multi",
    )(h, g, w_up, w_gate, w_dw, b_dw, w_down, p1, p2)


def _ple_kernel(h_ref, g_ref, p_ref, wg_ref, wp_ref, o_ref):
    h = h_ref[...]
    xn = (_rms_rows(h) * g_ref[...]).astype(BF16)
    gate = _dot(xn, wg_ref[...])
    proj = _dot(p_ref[...].astype(BF16), wp_ref[...])
    o_ref[...] = h + _sigmoid(gate) * proj


def _ple_call(h, g, p, w_gate, w_proj, tm):
    m, d = h.shape
    pd = p.shape[1]
    resident = pl.Buffered(1)
    return pl.pallas_call(
        _ple_kernel,
        grid=(m // tm,),
        in_specs=[
            pl.BlockSpec((tm, d), lambda i: (i, 0)),
            pl.BlockSpec((1, d), lambda i: (0, 0)),
            pl.BlockSpec((tm, pd), lambda i: (i, 0)),
            pl.BlockSpec((d, d), lambda i: (0, 0), pipeline_mode=resident),
            pl.BlockSpec((pd, d), lambda i: (0, 0), pipeline_mode=resident),
        ],
        out_specs=pl.BlockSpec((tm, d), lambda i: (i, 0)),
        out_shape=jax.ShapeDtypeStruct((m, d), F32),
        compiler_params=_params(1),
        name="ple",
    )(h, g, p, w_gate, w_proj)


def _qkv_kernel(h_ref, gkv_ref, gmix_ref, wk_ref, wv_ref, wq_ref, gk_ref, gq_ref,
                wf_ref, bf_ref, tri_ref, place_ref, *refs, tm, tn, tps, transposed):
    if transposed:
        (k_ref, v_ref, kb_ref, kx_ref, vt_ref, qt_ref, lf_ref,
         xkv_ref, xq_ref, carry_ref, cp_ref) = refs
    else:
        (k_ref, v_ref, kb_ref, vb_ref, q_ref, lf_ref, c_ref,
         xkv_ref, xq_ref, carry_ref) = refs
    i = pl.program_id(0)
    j = pl.program_id(1)

    @pl.when(j == 0)
    def _():
        r = _rms_rows(h_ref[...])
        xkv = (r * gkv_ref[...]).astype(BF16)
        xkv_ref[...] = xkv
        xq_ref[...] = (r * gmix_ref[...]).astype(BF16)
        lf = _log_sigmoid(_dot(xkv, wf_ref[...]) + bf_ref[...])
        lf_ref[...] = lf

        @pl.when((i % tps) == 0)
        def _():
            carry_ref[...] = jnp.zeros(carry_ref.shape, F32)

        c = _dot3_left(tri_ref[...], lf) + carry_ref[...]
        carry_ref[...] = c[tm - 1:tm, :]
        if transposed:
            for piece, val in enumerate(_split3(-LOG2_E * c)):
                cp_ref[piece] = val
        else:
            c_ref[...] = c

    xkv = xkv_ref[...]
    kk = _dot(xkv, wk_ref[...])
    vv = _dot(xkv, wv_ref[...])
    qq = _dot(xq_ref[...], wq_ref[...])
    v_ref[...] = vv
    scale = HEAD_DIM ** -0.5 * (LOG2_E if transposed else 1.0)
    qn_heads = []
    for c0 in range(0, tn, HEAD_DIM):
        sl = slice(c0, c0 + HEAD_DIM)
        kn = _rms_rows(kk[:, sl]) * gk_ref[...]
        k_ref[:, sl] = kn
        kb_ref[:, sl] = kn.astype(BF16)
        qn_heads.append(_rms_rows(qq[:, sl]) * gq_ref[...] * scale)
    qn = jnp.concatenate(qn_heads, axis=1)
    if transposed:
        vt_ref[0] = vv.T.astype(BF16)
        qt_ref[0] = qn.T.astype(BF16)
        kx = _dot(cp_ref[0], place_ref[0])
        for piece in range(1, N_SPLIT):
            kx = kx + _dot(cp_ref[piece], place_ref[piece])
        kx_ref[...] = kx.astype(BF16)
    else:
        vb_ref[...] = vv.astype(BF16)
        q_ref[...] = qn.astype(BF16)


def _qkv_call(h, g_kv, g_mix, w_k, w_v, w_q, g_k, g_q, w_f_pad, b_f_pad,
              tri, place, tm, tn, seq_len, transposed):
    m, d = h.shape
    tps = max(seq_len // tm, 1)
    kern = functools.partial(_qkv_kernel, tm=tm, tn=tn, tps=tps, transposed=transposed)
    row = lambda i, j: (i, 0)
    const = lambda i, j: (0, 0)
    col = lambda i, j: (0, j)
    tile = lambda i, j: (i, j)
    tile_t = lambda i, j: (i, j, 0)
    in_specs = [
        pl.BlockSpec((tm, d), row),
        pl.BlockSpec((1, d), const),
        pl.BlockSpec((1, d), const),
        pl.BlockSpec((d, tn), col),
        pl.BlockSpec((d, tn), col),
        pl.BlockSpec((d, tn), col),
        pl.BlockSpec((1, HEAD_DIM), const),
        pl.BlockSpec((1, HEAD_DIM), const),
        pl.BlockSpec((d, LANES), const),
        pl.BlockSpec((1, LANES), const),
        pl.BlockSpec((tm, tm), const),
        pl.BlockSpec((N_SPLIT, LANES, tn), lambda i, j: (0, 0, j)),
    ]
    scratch = [pltpu.VMEM((tm, d), BF16), pltpu.VMEM((tm, d), BF16),
               pltpu.VMEM((1, LANES), F32)]
    f32_md = jax.ShapeDtypeStruct((m, d), F32)
    bf_md = jax.ShapeDtypeStruct((m, d), BF16)
    if transposed:
        out_specs = [
            pl.BlockSpec((tm, tn), tile), pl.BlockSpec((tm, tn), tile),
            pl.BlockSpec((tm, tn), tile), pl.BlockSpec((tm, tn), tile),
            pl.BlockSpec((1, tn, tm), tile_t), pl.BlockSpec((1, tn, tm), tile_t),
            pl.BlockSpec((tm, LANES), row),
        ]
        bf_t = jax.ShapeDtypeStruct((m // tm, d, tm), BF16)
        out_shape = [f32_md, f32_md, bf_md, bf_md, bf_t, bf_t,
                     jax.ShapeDtypeStruct((m, LANES), F32)]
        scratch.append(pltpu.VMEM((N_SPLIT, tm, LANES), BF16))
    else:
        out_specs = [
            pl.BlockSpec((tm, tn), tile), pl.BlockSpec((tm, tn), tile),
            pl.BlockSpec((tm, tn), tile), pl.BlockSpec((tm, tn), tile),
            pl.BlockSpec((tm, tn), tile),
            pl.BlockSpec((tm, LANES), row), pl.BlockSpec((tm, LANES), row),
        ]
        out_shape = [f32_md, f32_md, bf_md, bf_md, bf_md,
                     jax.ShapeDtypeStruct((m, LANES), F32),
                     jax.ShapeDtypeStruct((m, LANES), F32)]
    return pl.pallas_call(
        kern,
        grid=(m // tm, d // tn),
        in_specs=in_specs,
        out_specs=out_specs,
        out_shape=out_shape,
        scratch_shapes=scratch,
        compiler_params=_params(2),
        name="qkv_t" if transposed else "qkv",
    )(h, g_kv, g_mix, w_k, w_v, w_q, g_k, g_q, w_f_pad, b_f_pad, tri, place)


def _attn_kernel(q_ref, k_ref, kx_ref, v_ref, o_ref,
                 s0_ref, s1_ref, bm0_ref, bm1_ref, m_ref, acc_ref, *, tq, nb):
    ones_rows = (lax.broadcasted_iota(jnp.int32, (HEAD_DIM, tq), 0) < N_SPLIT).astype(BF16)
    sum_rows = jnp.ones((acc_ref.shape[0] - HEAD_DIM, tq), BF16)
    s_refs = (s0_ref, s1_ref)
    bm_refs = (bm0_ref, bm1_ref)

    def scores(qi, kb, slot):
        q_aug = jnp.concatenate([q_ref[qi], ones_rows], axis=0)
        rows = slice(kb * tq, (kb + 1) * tq)
        k_aug = jnp.concatenate([k_ref[rows, :], kx_ref[rows, :]], axis=1)
        s = _dot(k_aug, q_aug)
        if kb == qi:
            key = lax.broadcasted_iota(jnp.int32, (tq, tq), 0)
            qry = lax.broadcasted_iota(jnp.int32, (tq, tq), 1)
            s = jnp.where(key <= qry, s, NEG_INF)
        s_refs[slot][...] = s
        bm_refs[slot][...] = jnp.max(s, axis=0, keepdims=True)

    def accumulate(qi, kb, slot):
        v_aug = jnp.concatenate([v_ref[kb], sum_rows], axis=0)
        if kb == 0:
            m_new = bm_refs[slot][...]
            p = jnp.exp2((s_refs[slot][...] - m_new).astype(BF16))
            acc_ref[...] = _dot(v_aug, p)
        else:
            m_old = m_ref[...]
            m_new = jnp.maximum(m_old, bm_refs[slot][...])
            alpha = jnp.exp2(m_old - m_new)
            p = jnp.exp2((s_refs[slot][...] - m_new).astype(BF16))
            acc_ref[...] = alpha * acc_ref[...] + _dot(v_aug, p)
        m_ref[...] = m_new
        if kb == qi:
            out = acc_ref[0:HEAD_DIM, :] / acc_ref[HEAD_DIM:HEAD_DIM + 1, :]
            o_ref[qi * tq:(qi + 1) * tq, :] = out.T.astype(o_ref.dtype)

    pairs = [(qi, kb) for qi in range(nb) for kb in range(qi + 1)]
    scores(*pairs[0], 0)
    for i, (qi, kb) in enumerate(pairs):
        if i + 1 < len(pairs):
            scores(*pairs[i + 1], (i + 1) % 2)
        accumulate(qi, kb, i % 2)


def _attn_call(q_t, k, kx, v_t, n_seq, seq_len, tq):
    d = k.shape[1]
    nh = d // HEAD_DIM
    nb = seq_len // tq
    kern = functools.partial(_attn_kernel, tq=tq, nb=nb)
    return pl.pallas_call(
        kern,
        grid=(n_seq, nh),
        in_specs=[
            pl.BlockSpec((nb, HEAD_DIM, tq), lambda n, h: (n, h, 0)),
            pl.BlockSpec((seq_len, HEAD_DIM), lambda n, h: (n, h)),
            pl.BlockSpec((seq_len, HEAD_DIM), lambda n, h: (n, h)),
            pl.BlockSpec((nb, HEAD_DIM, tq), lambda n, h: (n, h, 0)),
        ],
        out_specs=pl.BlockSpec((seq_len, HEAD_DIM), lambda n, h: (n, h)),
        out_shape=jax.ShapeDtypeStruct((n_seq * seq_len, d), BF16),
        scratch_shapes=[pltpu.VMEM((tq, tq), F32), pltpu.VMEM((tq, tq), F32),
                        pltpu.VMEM((1, tq), F32), pltpu.VMEM((1, tq), F32),
                        pltpu.VMEM((1, tq), F32),
                        pltpu.VMEM((HEAD_DIM + 2 * SUBLANES, tq), F32)],
        compiler_params=_params(2),
        name="fox_prompt",
    )(q_t, k, kx, v_t)


def _bias_kernel(pt_ref, *refs, n_steps):
    nb = BIAS_PAGES_PER_STEP
    lf_refs = refs[0:nb]
    sfx_ref, tot_ref, later_ref, spread_ref, o_ref, x_ref = refs[nb:]
    s = pl.program_id(1)
    base = pl.multiple_of(s * nb, nb)
    for e in range(nb):
        x_ref[pl.ds(base + e, 1), :] = lf_refs[e][0]

    @pl.when(s == n_steps - 1)
    def _():
        x = x_ref[...]
        within = _dot3(x, sfx_ref[...])
        tot = _dot3(x, tot_ref[...])
        later = _dot3_left(later_ref[...], tot)
        o_ref[0] = within + _dot3(later, spread_ref[...])


def _bias_call(page_table, lf_flat, sfx, tot_m, later_m, spread):
    b, n_pages = page_table.shape
    width = lf_flat.shape[2]
    nb = BIAS_PAGES_PER_STEP
    n_steps = n_pages // nb

    def page_map(e):
        return lambda bi, s, pt: (pt[bi, s * nb + e], 0, 0)

    const2 = lambda bi, s, pt: (0, 0)
    in_specs = [pl.BlockSpec((1, 1, width), page_map(e)) for e in range(nb)]
    in_specs += [pl.BlockSpec(sfx.shape, const2), pl.BlockSpec(tot_m.shape, const2),
                 pl.BlockSpec(later_m.shape, const2), pl.BlockSpec(spread.shape, const2)]
    grid_spec = pltpu.PrefetchScalarGridSpec(
        num_scalar_prefetch=1,
        grid=(b, n_steps),
        in_specs=in_specs,
        out_specs=pl.BlockSpec((1, n_pages, width), lambda bi, s, pt: (bi, 0, 0)),
        scratch_shapes=[pltpu.VMEM((n_pages, width), F32)],
    )
    return pl.pallas_call(
        functools.partial(_bias_kernel, n_steps=n_steps),
        grid_spec=grid_spec,
        out_shape=jax.ShapeDtypeStruct((b, n_pages, width), F32),
        compiler_params=_params(2),
        name="decode_bias",
    )(page_table, *([lf_flat] * nb), sfx, tot_m, later_m, spread)


def _sattn_kernel(pt_ref, q_ref, *refs, n_steps, n_grp):
    g_pages = PAGES_PER_STEP
    k_refs = refs[0:g_pages]
    v_refs = refs[g_pages:2 * g_pages]
    (bias_ref, mask_ref, kn_ref, vn_ref, bn_ref, maskn_ref, o_ref,
     m_ref, l_ref, acc_ref) = refs[2 * g_pages:]
    g = pl.program_id(1)
    keys_g = mask_ref.shape[1]
    dh = q_ref.shape[3]

    @pl.when(g == 0)
    def _():
        m_ref[...] = jnp.full(m_ref.shape, NEG_INF, F32)
        l_ref[...] = jnp.zeros(l_ref.shape, F32)
        acc_ref[...] = jnp.zeros(acc_ref.shape, F32)

    def update(gi, s, v_rows):
        m_old = m_ref[gi]
        m_new = jnp.maximum(m_old, jnp.max(s, axis=-1, keepdims=True))
        alpha = jnp.exp(m_old - m_new)
        p = jnp.exp(s - m_new)
        l_ref[gi] = alpha * l_ref[gi] + jnp.sum(p, axis=-1, keepdims=True)
        acc_ref[gi] = alpha * acc_ref[gi] + _dot(p.astype(BF16), v_rows)
        m_ref[gi] = m_new

    def group_rows(page_ref, gi):
        tiles = page_ref[0, pl.ds(gi, PAGE_SIZE, stride=n_grp), :, :]
        return tiles.reshape(keys_g, dh).astype(BF16)

    steps_per_block = SUBLANES // g_pages
    bias_rows = [bias_ref[0, pl.ds((g % steps_per_block) * g_pages + e, 1), :]
                 for e in range(g_pages)]

    def group_scores(gi):
        parts = [_dot_nt(q_ref[0, gi], group_rows(k_refs[e], gi))
                 + (mask_ref[...] + bias_rows[e][:, gi * keys_g:(gi + 1) * keys_g])
                 for e in range(g_pages)]
        return jnp.concatenate(parts, axis=1)

    scores = [group_scores(gi) for gi in range(n_grp)]
    for gi in range(n_grp):
        values = jnp.concatenate([group_rows(v_refs[e], gi) for e in range(g_pages)], axis=0)
        update(gi, scores[gi], values)

    @pl.when(g == n_steps - 1)
    def _():
        for gi in range(n_grp):
            s = _dot_nt(q_ref[0, gi], kn_ref[0, gi]) + (maskn_ref[...] + bn_ref[0, gi])
            update(gi, s, vn_ref[0, gi])
        o_ref[0] = (acc_ref[...] / l_ref[...]).astype(o_ref.dtype)


def _sattn_call(page_table, q4, cache_k, cache_v, bias, mask, k_new, v_new,
                bias_new, mask_new):
    b, n_grp, rows, dh = q4.shape
    n_pages = page_table.shape[1]
    n_steps = n_pages // PAGES_PER_STEP
    page_tiles = cache_k.shape[1]
    width = bias.shape[2]
    steps_per_block = SUBLANES // PAGES_PER_STEP

    def page_map(e):
        return lambda bi, g, pt: (pt[bi, g * PAGES_PER_STEP + e], 0, 0, 0)

    per_seq = lambda bi, g, pt: (bi, 0, 0, 0)
    const2 = lambda bi, g, pt: (0, 0)
    page_spec = lambda e: pl.BlockSpec((1, page_tiles, SUBLANES, dh), page_map(e))
    in_specs = [pl.BlockSpec((1, n_grp, rows, dh), per_seq)]
    in_specs += [page_spec(e) for e in range(PAGES_PER_STEP)]
    in_specs += [page_spec(e) for e in range(PAGES_PER_STEP)]
    in_specs += [
        pl.BlockSpec((1, SUBLANES, width), lambda bi, g, pt: (bi, g // steps_per_block, 0)),
        pl.BlockSpec(mask.shape, const2),
        pl.BlockSpec((1, n_grp, rows, dh), per_seq),
        pl.BlockSpec((1, n_grp, rows, dh), per_seq),
        pl.BlockSpec((1, n_grp, 1, rows), per_seq),
        pl.BlockSpec(mask_new.shape, const2),
    ]
    grid_spec = pltpu.PrefetchScalarGridSpec(
        num_scalar_prefetch=1,
        grid=(b, n_steps),
        in_specs=in_specs,
        out_specs=pl.BlockSpec((1, n_grp, rows, dh), per_seq),
        scratch_shapes=[pltpu.VMEM((n_grp, rows, 1), F32), pltpu.VMEM((n_grp, rows, 1), F32),
                        pltpu.VMEM((n_grp, rows, dh), F32)],
    )
    kern = functools.partial(_sattn_kernel, n_steps=n_steps, n_grp=n_grp)
    return pl.pallas_call(
        kern,
        grid_spec=grid_spec,
        out_shape=jax.ShapeDtypeStruct((b, n_grp, rows, dh), BF16),
        compiler_params=_params(2),
        name="fox_decode",
    )(page_table, q4, *([cache_k] * PAGES_PER_STEP), *([cache_v] * PAGES_PER_STEP),
      bias, mask, k_new, v_new, bias_new, mask_new)


def _oproj_kernel(a_ref, w_ref, h_ref, o_ref):
    o_ref[...] = h_ref[...] + _dot(a_ref[...], w_ref[...])


def _oproj_call(a, w_o, h, tm, tn):
    m, d = h.shape
    return pl.pallas_call(
        _oproj_kernel,
        grid=(m // tm, d // tn),
        in_specs=[
            pl.BlockSpec((tm, d), lambda i, j: (i, 0)),
            pl.BlockSpec((d, tn), lambda i, j: (0, j)),
            pl.BlockSpec((tm, tn), lambda i, j: (i, j)),
        ],
        out_specs=pl.BlockSpec((tm, tn), lambda i, j: (i, j)),
        out_shape=jax.ShapeDtypeStruct((m, d), F32),
        compiler_params=_params(2),
        name="o_proj",
    )(a, w_o, h)


def _cumsum_matrix(n, group):
    t = np.arange(n)[:, None]
    tp = np.arange(n)[None, :]
    return ((tp <= t) & (tp // group == t // group)).astype(BF16)


def _placement(nh, d):
    lane = np.arange(LANES)[None, :, None]
    col = np.arange(d)[None, None, :]
    piece = np.arange(N_SPLIT)[:, None, None]
    return ((lane < nh) & (col == lane * HEAD_DIM + piece)).astype(BF16)


def kernel(x_prompt, x_sample, state_conv_a, state_ffn, cache_k, cache_v, cache_logf, page_table, p_prompt, p_sample, g_mix_a, w_pw1, w_dwa, b_dwa, ln_g, ln_b, w_pw2, g_kv, w_k, w_v, g_k, w_f, b_f, g_mix_b, w_q, g_q, w_o, g_ffn, w_up, w_gate, w_fdw, b_fdw, w_down, g_ple, w_ple_gate, w_ple_proj):
    n_p, t_p, d = x_prompt.shape
    n_s, t_s, _ = x_sample.shape
    depth = g_ffn.shape[0]
    n_a = g_mix_a.shape[0]
    f = w_up.shape[2]
    nh = d // HEAD_DIM
    m_s = n_s * t_s

    bf = lambda w: w.astype(BF16)
    layers = lambda w: [bf(w[i]) for i in range(w.shape[0])]
    w_pw1_b, w_pw2_b = layers(w_pw1), layers(w_pw2)
    w_k_b, w_v_b, w_q_b, w_o_b = bf(w_k), bf(w_v), layers(w_q), layers(w_o)
    w_up_b, w_gate_b, w_down_b = layers(w_up), layers(w_gate), layers(w_down)
    w_pg_b, w_pp_b = layers(w_ple_gate), layers(w_ple_proj)
    w_f_pad = jnp.pad(bf(w_f), ((0, 0), (0, LANES - nh)))
    b_f_pad = jnp.pad(b_f, (0, LANES - nh)).reshape(1, LANES)
    place = _placement(nh, d)
    row = lambda a: a.reshape(1, -1)

    def trunk(x, p, n_seq, seq_len, a_prev32, tiles, ffn, attend, tri, transposed):
        h = x.reshape(n_seq * seq_len, d)
        a_us, f_bufs = [], []
        k = v = lf = None
        for i in range(depth):
            if i < n_a:
                u = _glu_call(h, row(g_mix_a[i]), w_pw1_b[i], tiles.rows, tiles.cols)
                h = _conva_call(u, a_prev32[i], w_dwa[i], row(b_dwa[i]), row(ln_g[i]),
                                row(ln_b[i]), w_pw2_b[i], h, tiles.rows_conv, seq_len)
                a_us.append(u.reshape(n_seq, seq_len, d))
            else:
                jb = i - n_a
                outs = _qkv_call(
                    h, row(g_kv), row(g_mix_b[jb]), w_k_b, w_v_b, w_q_b[jb],
                    row(g_k), row(g_q[jb]), w_f_pad, b_f_pad, tri, place,
                    tiles.rows_qkv, tiles.cols, seq_len, transposed)
                k, v = outs[0], outs[1]
                lf = outs[6] if transposed else outs[5]
                a = attend(outs)
                h = _oproj_call(a, w_o_b[jb], h, tiles.rows, tiles.cols_oproj)
            h, fb = ffn(i, h)
            f_bufs.append(fb)
            h = _ple_call(h, row(g_ple[i]), p[i].reshape(n_seq * seq_len, -1),
                          w_pg_b[i], w_pp_b[i], tiles.rows_ple)
        shp = (n_seq, seq_len, nh, HEAD_DIM)
        return (h.reshape(n_seq, seq_len, d), a_us, jnp.stack(f_bufs),
                k.reshape(shp), v.reshape(shp),
                lf[:, :nh].reshape(n_seq, seq_len, nh))

    tp = PROMPT_TILES

    def ffn_prompt(i, h):
        prev = jnp.zeros((n_p, CONV_F_WIDTH - 1, f), F32)
        return _ffn_seq_call(h, row(g_ffn[i]), w_up_b[i], w_gate_b[i], w_fdw[i],
                             row(b_fdw[i]), w_down_b[i], prev, tp.rows_ffn, tp.cols, t_p)

    def attend_prompt(outs):
        _, _, kb, kx, v_t, q_t, _ = outs
        return _attn_call(q_t, kb, kx, v_t, n_p, t_p, tp.rows_qkv)

    a_prev_p = [jnp.zeros((n_p, HALO_ROWS, d), F32) for _ in range(n_a)]
    (y_p, a_us_p, ffn_p, k_p, v_p, lf_p) = trunk(
        x_prompt, p_prompt, n_p, t_p, a_prev_p, tp, ffn_prompt,
        attend_prompt, _cumsum_matrix(tp.rows_qkv, tp.rows_qkv), True)
    conv_a_p = jnp.stack([u[:, t_p - (CONV_A_WIDTH - 1):] for u in a_us_p])

    def ffn_sample(i, h):
        st = state_ffn[i]
        z = jnp.zeros((n_s, t_s, f), F32)
        p1 = z.at[:, 0].set(st[:, 1]).reshape(m_s, f)
        p2 = z.at[:, 0].set(st[:, 0]).at[:, 1].set(st[:, 1]).reshape(m_s, f)
        h, u = _ffn_multi_call(h, row(g_ffn[i]), w_up_b[i], w_gate_b[i], w_fdw[i],
                               row(b_fdw[i]), w_down_b[i], p1, p2, tp.cols, t_s)
        full = jnp.concatenate([st, u.reshape(n_s, t_s, f)], axis=1)
        return h, full[:, -(CONV_F_WIDTH - 1):]

    n_pool = cache_k.shape[0]
    n_pages = page_table.shape[1]
    n_grp = nh // SUBLANES
    keys_g = PAGE_SIZE * SUBLANES
    page_rows = PAGE_SIZE * nh
    tiles_shape = (n_pool, PAGE_SIZE * n_grp, SUBLANES, HEAD_DIM)
    cache_k4 = cache_k.reshape(tiles_shape)
    cache_v4 = cache_v.reshape(tiles_shape)
    lf_flat = cache_logf.reshape(n_pool, 1, page_rows)
    fr = np.arange(page_rows)
    t_in, h_in = fr // nh, fr % nh
    t_out = (fr % keys_g) // SUBLANES
    h_out = (fr // keys_g) * SUBLANES + fr % SUBLANES
    lane = np.arange(LANES)
    sfx = ((h_in[:, None] == h_out[None, :]) & (t_in[:, None] > t_out[None, :])).astype(BF16)
    tot_m = (h_in[:, None] == lane[None, :]).astype(BF16)
    pg = np.arange(n_pages)
    later_m = (pg[None, :] > pg[:, None]).astype(BF16)
    spread = (lane[:, None] == h_out[None, :]).astype(BF16)
    qrow = np.arange(t_s * SUBLANES)
    kcol = np.arange(keys_g)
    mask = np.where((qrow[:, None] % SUBLANES) == (kcol[None, :] % SUBLANES),
                    0.0, NEG_INF).astype(np.float32)
    mask_new = np.where(((qrow[:, None] % SUBLANES) == (qrow[None, :] % SUBLANES))
                        & (qrow[None, :] // SUBLANES <= qrow[:, None] // SUBLANES),
                        0.0, NEG_INF).astype(np.float32)
    bias = _bias_call(page_table, lf_flat, sfx, tot_m, later_m, spread)

    def by_group(a, width):
        a5 = a.reshape(n_s, t_s, n_grp, SUBLANES, width)
        return jnp.transpose(a5, (0, 2, 1, 3, 4)).reshape(n_s, n_grp, t_s * SUBLANES, width)

    def attend_sample(outs):
        _, _, kb, vb, q, _, c = outs
        bias_new = -by_group(c[:, :nh], 1).reshape(n_s, n_grp, 1, t_s * SUBLANES)
        a = _sattn_call(page_table, by_group(q, HEAD_DIM), cache_k4, cache_v4, bias, mask,
                        by_group(kb, HEAD_DIM), by_group(vb, HEAD_DIM), bias_new, mask_new)
        a5 = a.reshape(n_s, n_grp, t_s, SUBLANES, HEAD_DIM)
        return jnp.transpose(a5, (0, 2, 1, 3, 4)).reshape(m_s, d)

    a_prev_s = [jnp.pad(state_conv_a[i], ((0, 0), (HALO_ROWS - (CONV_A_WIDTH - 1), 0), (0, 0)))
                for i in range(n_a)]
    ts = _Tiles(rows=m_s, rows_qkv=m_s, rows_ffn=m_s, rows_conv=t_s, rows_ple=m_s,
                cols=tp.cols, cols_oproj=tp.cols)
    (y_s, a_us_s, ffn_s, k_s, v_s, lf_s) = trunk(
        x_sample, p_sample, n_s, t_s, a_prev_s, ts, ffn_sample,
        attend_sample, _cumsum_matrix(m_s, t_s), False)
    conv_a_s = jnp.stack([
        jnp.concatenate([state_conv_a[i], a_us_s[i]], axis=1)[:, -(CONV_A_WIDTH - 1):]
        for i in range(n_a)])

    return (y_p, y_s, conv_a_p, ffn_p, k_p, v_p, lf_p,
            conv_a_s, ffn_s, k_s, v_s, lf_s)
```

```python
import functools
from typing import NamedTuple

import jax
import jax.numpy as jnp
import numpy as np
from jax import lax
from jax.experimental import pallas as pl
from jax.experimental.pallas import tpu as pltpu

F32 = jnp.float32
BF16 = jnp.bfloat16

HEAD_DIM = 128
CONV_A_WIDTH = 31
CONV_F_WIDTH = 3
PAGE_SIZE = 128
EPS = 1e-6
NEG_INF = -1e30
LOG2_E = 1.4426950408889634

LANES = 128
SUBLANES = 8
HALO_ROWS = 32
VMEM_LIMIT = 56 * 1024 * 1024
PAGES_PER_STEP = 8
BIAS_PAGES_PER_STEP = 16
N_SPLIT = 3


class _Tiles(NamedTuple):
    rows: int
    rows_qkv: int
    rows_ffn: int
    rows_conv: int
    rows_ple: int
    cols: int


PROMPT_TILES = _Tiles(rows=1024, rows_qkv=512, rows_ffn=512, rows_conv=256,
                      rows_ple=512, cols=512)


def _params(n_axes):
    return pltpu.CompilerParams(
        dimension_semantics=("arbitrary",) * n_axes,
        vmem_limit_bytes=VMEM_LIMIT)


def _dot(a, b):
    return jnp.dot(a, b, preferred_element_type=F32)


def _dot_nt(a, b):
    return lax.dot_general(a, b, (((1,), (1,)), ((), ())),
                           preferred_element_type=F32)


def _sigmoid(x):
    return jax.nn.sigmoid(x)


def _log_sigmoid(z):
    return -(jnp.maximum(-z, 0.0) + jnp.log1p(jnp.exp(-jnp.abs(z))))


def _rms_rows(x):
    ms = jnp.mean(x * x, axis=-1, keepdims=True)
    return x * lax.rsqrt(ms + EPS)


def _split3(x):
    hi = x.astype(BF16)
    r1 = x - hi.astype(F32)
    mid = r1.astype(BF16)
    lo = (r1 - mid.astype(F32)).astype(BF16)
    return hi, mid, lo


def _dot3(x, w_bf16):
    hi, mid, lo = _split3(x)
    return _dot(hi, w_bf16) + _dot(mid, w_bf16) + _dot(lo, w_bf16)


def _dot3_left(w_bf16, x):
    hi, mid, lo = _split3(x)
    return _dot(w_bf16, hi) + _dot(w_bf16, mid) + _dot(w_bf16, lo)


def _glu_kernel(x_ref, g_ref, w1_ref, w2_ref, u_ref, xn_ref):
    @pl.when(pl.program_id(1) == 0)
    def _():
        xn_ref[...] = (_rms_rows(x_ref[...]) * g_ref[...]).astype(BF16)

    xn = xn_ref[...]
    a1 = _dot(xn, w1_ref[...])
    a2 = _dot(xn, w2_ref[...])
    u_ref[...] = a1 * _sigmoid(a2)


def _glu_call(h, g, w_pw1, tm, tn):
    m, d = h.shape
    nj = d // tn
    return pl.pallas_call(
        _glu_kernel,
        grid=(m // tm, nj),
        in_specs=[
            pl.BlockSpec((tm, d), lambda i, j: (i, 0)),
            pl.BlockSpec((1, d), lambda i, j: (0, 0)),
            pl.BlockSpec((d, tn), lambda i, j: (0, j)),
            pl.BlockSpec((d, tn), lambda i, j: (0, j + nj)),
        ],
        out_specs=pl.BlockSpec((tm, tn), lambda i, j: (i, j)),
        out_shape=jax.ShapeDtypeStruct((m, d), F32),
        scratch_shapes=[pltpu.VMEM((tm, d), BF16)],
        compiler_params=_params(2),
        name="glu",
    )(h, g, w_pw1, w_pw1)


def _conva_kernel(u_ref, halo_ref, prev_ref, wdw_ref, bdw_ref, lng_ref, lnb_ref,
                  w2_ref, h_ref, o_ref, full_ref, c_ref, *, tm, tps, rb):
    i = pl.program_id(0)
    d = u_ref.shape[1]
    first = (i % tps) == 0
    full_ref[0:HALO_ROWS, :] = jnp.where(first, prev_ref[0], halo_ref[...])
    full_ref[HALO_ROWS:HALO_ROWS + tm, :] = u_ref[...]
    full_ref[HALO_ROWS + tm:HALO_ROWS + tm + SUBLANES, :] = jnp.zeros((SUBLANES, d), F32)
    off = HALO_ROWS - (CONV_A_WIDTH - 1)

    def conv_chunk(r0, c0):
        cols = slice(c0, c0 + LANES)
        acc = jnp.broadcast_to(bdw_ref[:, cols], (rb, LANES))
        for b in range(SUBLANES):
            z = None
            for j in range(CONV_A_WIDTH):
                if (off + j) % SUBLANES != b:
                    continue
                s = r0 + (off + j) // SUBLANES * SUBLANES
                term = wdw_ref[j:j + 1, cols] * full_ref[s:s + rb + SUBLANES, cols]
                z = term if z is None else z + term
            acc = acc + z[b:b + rb, :]
        c_ref[r0:r0 + rb, cols] = acc

    for r0 in range(0, tm, rb):
        for c0 in range(0, d, LANES):
            conv_chunk(r0, c0)
    c = c_ref[...]
    mu = jnp.mean(c, axis=-1, keepdims=True)
    xc = c - mu
    var = jnp.mean(xc * xc, axis=-1, keepdims=True)
    y = xc * lax.rsqrt(var + EPS) * lng_ref[...] + lnb_ref[...]
    a = (y * _sigmoid(y)).astype(BF16)
    o_ref[...] = h_ref[...] + _dot(a, w2_ref[...])


def _conva_call(u, prev32, w_dw, b_dw, ln_g, ln_b, w_pw2, h, tm, seq_len):
    m, d = u.shape
    tps = max(seq_len // tm, 1)
    rb = min(tm, 128)
    if tm % HALO_ROWS == 0:
        k = tm // HALO_ROWS
        halo_map = lambda i: (jnp.maximum(i * k - 1, 0), 0)
    else:
        assert tps == 1
        halo_map = lambda i: (0, 0)
    kern = functools.partial(_conva_kernel, tm=tm, tps=tps, rb=rb)
    return pl.pallas_call(
        kern,
        grid=(m // tm,),
        in_specs=[
            pl.BlockSpec((tm, d), lambda i: (i, 0)),
            pl.BlockSpec((HALO_ROWS, d), halo_map),
            pl.BlockSpec((1, HALO_ROWS, d), lambda i: (i // tps, 0, 0)),
            pl.BlockSpec((CONV_A_WIDTH, d), lambda i: (0, 0)),
            pl.BlockSpec((1, d), lambda i: (0, 0)),
            pl.BlockSpec((1, d), lambda i: (0, 0)),
            pl.BlockSpec((1, d), lambda i: (0, 0)),
            pl.BlockSpec((d, d), lambda i: (0, 0), pipeline_mode=pl.Buffered(1)),
            pl.BlockSpec((tm, d), lambda i: (i, 0)),
        ],
        out_specs=pl.BlockSpec((tm, d), lambda i: (i, 0)),
        out_shape=jax.ShapeDtypeStruct((m, d), F32),
        scratch_shapes=[pltpu.VMEM((HALO_ROWS + tm + SUBLANES, d), F32),
                        pltpu.VMEM((tm, d), F32)],
        compiler_params=_params(1),
        name="conv_a",
    )(u, u, prev32, w_dw, b_dw, ln_g, ln_b, w_pw2, h)


def _ffn_body(h_ref, g_ref, wu_ref, wg_ref, wdw_ref, bdw_ref, wd_ref,
              o_ref, xn_ref, ubuf_ref, tm, prev_rows, fix_shifted):
    xn = xn_ref[...]
    u = _dot(xn, wu_ref[...])
    ubuf_ref[SUBLANES:SUBLANES + tm, :] = u
    ubuf_ref[SUBLANES - 2:SUBLANES, :] = prev_rows
    u1 = ubuf_ref[SUBLANES - 1:SUBLANES - 1 + tm, :]
    u2 = ubuf_ref[SUBLANES - 2:SUBLANES - 2 + tm, :]
    u1, u2 = fix_shifted(u1, u2)
    c = (wdw_ref[0:1, :] * u2 + wdw_ref[1:2, :] * u1 + wdw_ref[2:3, :] * u
         + bdw_ref[...])
    gt = _dot(xn, wg_ref[...])
    act = (c * _sigmoid(c) * gt).astype(BF16)
    o_ref[...] += _dot(act, wd_ref[...])


def _ffn_seq_kernel(h_ref, g_ref, wu_ref, wg_ref, wdw_ref, bdw_ref, wd_ref,
                    *rest, tm, tps, with_attn):
    if with_attn:
        a_ref, wo_ref, *rest = rest
    prev_ref, o_ref, fb_ref, xn_ref, ubuf_ref, carry_ref = rest
    i = pl.program_id(0)
    j = pl.program_id(1)
    first = (i % tps) == 0

    @pl.when(j == 0)
    def _():
        h = h_ref[...]
        if with_attn:
            h = h + _dot(a_ref[...], wo_ref[...])
        xn_ref[...] = (_rms_rows(h) * g_ref[...]).astype(BF16)
        o_ref[...] = h

        @pl.when(i == 0)
        def _():
            carry_ref[...] = jnp.zeros(carry_ref.shape, F32)

    prev_rows = jnp.where(first, prev_ref[0], carry_ref[j])
    _ffn_body(h_ref, g_ref, wu_ref, wg_ref, wdw_ref, bdw_ref, wd_ref,
              o_ref, xn_ref, ubuf_ref, tm, prev_rows, lambda a, b: (a, b))
    tail = ubuf_ref[SUBLANES + tm - 2:SUBLANES + tm, :]
    carry_ref[j] = tail

    @pl.when((i % tps) == tps - 1)
    def _():
        fb_ref[i // tps, j] = tail


def _ffn_multi_kernel(h_ref, g_ref, wu_ref, wg_ref, wdw_ref, bdw_ref, wd_ref,
                      p1_ref, p2_ref, o_ref, u_ref, xn_ref, ubuf_ref,
                      *, tm, seq_len):
    @pl.when(pl.program_id(1) == 0)
    def _():
        xn_ref[...] = (_rms_rows(h_ref[...]) * g_ref[...]).astype(BF16)
        o_ref[...] = h_ref[...]
        ubuf_ref[0:SUBLANES, :] = jnp.zeros((SUBLANES, ubuf_ref.shape[1]), F32)

    def fix_shifted(u1, u2):
        t = lax.broadcasted_iota(jnp.int32, (tm, 1), 0) % seq_len
        return (jnp.where(t >= 1, u1, p1_ref[...]),
                jnp.where(t >= 2, u2, p2_ref[...]))

    prev_rows = jnp.zeros((CONV_F_WIDTH - 1, ubuf_ref.shape[1]), F32)
    _ffn_body(h_ref, g_ref, wu_ref, wg_ref, wdw_ref, bdw_ref, wd_ref,
              o_ref, xn_ref, ubuf_ref, tm, prev_rows, fix_shifted)
    u_ref[...] = ubuf_ref[SUBLANES:SUBLANES + tm, :]


def _ffn_common_specs(tm, tn, d, f, layer):
    return [
        pl.BlockSpec((tm, d), lambda i, j: (i, 0)),
        pl.BlockSpec((1, d), lambda i, j: (0, 0)),
        pl.BlockSpec((None, d, tn), lambda i, j: (layer, 0, j)),
        pl.BlockSpec((None, d, tn), lambda i, j: (layer, 0, j)),
        pl.BlockSpec((CONV_F_WIDTH, tn), lambda i, j: (0, j)),
        pl.BlockSpec((1, tn), lambda i, j: (0, j)),
        pl.BlockSpec((None, tn, d), lambda i, j: (layer, j, 0)),
    ]


def _ffn_seq_call(h, g, w_up, w_gate, w_dw, b_dw, w_down, prev, tm, tn, seq_len, layer,
                  attn=None):
    m, d = h.shape
    f = w_up.shape[2]
    n_seq = m // seq_len
    tps = seq_len // tm
    nj = f // tn
    kern = functools.partial(_ffn_seq_kernel, tm=tm, tps=tps, with_attn=attn is not None)
    attn_specs = [] if attn is None else [
        pl.BlockSpec((tm, d), lambda i, j: (i, 0)),
        pl.BlockSpec((d, d), lambda i, j: (0, 0), pipeline_mode=pl.Buffered(1)),
    ]
    h_new, fb = pl.pallas_call(
        kern,
        grid=(m // tm, nj),
        in_specs=_ffn_common_specs(tm, tn, d, f, layer) + attn_specs + [
            pl.BlockSpec((1, CONV_F_WIDTH - 1, tn), lambda i, j: (i // tps, 0, j)),
        ],
        out_specs=[
            pl.BlockSpec((tm, d), lambda i, j: (i, 0)),
            pl.BlockSpec((n_seq, nj, CONV_F_WIDTH - 1, tn), lambda i, j: (0, 0, 0, 0)),
        ],
        out_shape=[jax.ShapeDtypeStruct((m, d), F32),
                   jax.ShapeDtypeStruct((n_seq, nj, CONV_F_WIDTH - 1, tn), F32)],
        scratch_shapes=[pltpu.VMEM((tm, d), BF16),
                        pltpu.VMEM((SUBLANES + tm, tn), F32),
                        pltpu.VMEM((nj, CONV_F_WIDTH - 1, tn), F32)],
        compiler_params=_params(2),
        name="ffn_seq",
    )(h, g, w_up, w_gate, w_dw, b_dw, w_down, *(attn or ()), prev)
    fb = jnp.transpose(fb, (0, 2, 1, 3)).reshape(n_seq, CONV_F_WIDTH - 1, f)
    return h_new, fb


def _ffn_multi_call(h, g, w_up, w_gate, w_dw, b_dw, w_down, p1, p2, tn, seq_len, layer):
    m, d = h.shape
    f = w_up.shape[2]
    kern = functools.partial(_ffn_multi_kernel, tm=m, seq_len=seq_len)
    return pl.pallas_call(
        kern,
        grid=(1, f // tn),
        in_specs=_ffn_common_specs(m, tn, d, f, layer) + [
            pl.BlockSpec((m, tn), lambda i, j: (0, j)),
            pl.BlockSpec((m, tn), lambda i, j: (0, j)),
        ],
        out_specs=[
            pl.BlockSpec((m, d), lambda i, j: (0, 0)),
            pl.BlockSpec((m, tn), lambda i, j: (0, j)),
        ],
        out_shape=[jax.ShapeDtypeStruct((m, d), F32),
                   jax.ShapeDtypeStruct((m, f), F32)],
        scratch_shapes=[pltpu.VMEM((m, d), BF16),
                        pltpu.VMEM((SUBLANES + m, tn), F32)],
        compiler_params=_params(2),
        name="ffn_multi",
    )(h, g, w_up, w_gate, w_dw, b_dw, w_down, p1, p2)


def _ple_kernel(h_ref, g_ref, p_ref, wg_ref, wp_ref, o_ref):
    h = h_ref[...]
    xn = (_rms_rows(h) * g_ref[...]).astype(BF16)
    gate = _dot(xn, wg_ref[...])
    proj = _dot(p_ref[...].astype(BF16), wp_ref[...])
    o_ref[...] = h + _sigmoid(gate) * proj


def _ple_call(h, g, p, w_gate, w_proj, tm, layer):
    m, d = h.shape
    pd = p.shape[1]
    resident = pl.Buffered(1)
    return pl.pallas_call(
        _ple_kernel,
        grid=(m // tm,),
        in_specs=[
            pl.BlockSpec((tm, d), lambda i: (i, 0)),
            pl.BlockSpec((1, d), lambda i: (0, 0)),
            pl.BlockSpec((tm, pd), lambda i: (i, 0)),
            pl.BlockSpec((None, d, d), lambda i: (layer, 0, 0), pipeline_mode=resident),
            pl.BlockSpec((None, pd, d), lambda i: (layer, 0, 0), pipeline_mode=resident),
        ],
        out_specs=pl.BlockSpec((tm, d), lambda i: (i, 0)),
        out_shape=jax.ShapeDtypeStruct((m, d), F32),
        compiler_params=_params(1),
        name="ple",
    )(h, g, p, w_gate, w_proj)


def _qkv_kernel(h_ref, gkv_ref, gmix_ref, wk_ref, wv_ref, wq_ref, gk_ref, gq_ref,
                wf_ref, bf_ref, tri_ref, place_ref, *refs, tm, tn, tps, transposed):
    if transposed:
        (k_ref, v_ref, kb_ref, kx_ref, vt_ref, qt_ref, lf_ref,
         xkv_ref, xq_ref, carry_ref, cp_ref) = refs
    else:
        (k_ref, v_ref, kb_ref, vb_ref, q_ref, lf_ref, c_ref,
         xkv_ref, xq_ref, carry_ref) = refs
    i = pl.program_id(0)
    j = pl.program_id(1)

    @pl.when(j == 0)
    def _():
        r = _rms_rows(h_ref[...])
        xkv = (r * gkv_ref[...]).astype(BF16)
        xkv_ref[...] = xkv
        xq_ref[...] = (r * gmix_ref[...]).astype(BF16)
        lf = _log_sigmoid(_dot(xkv, wf_ref[...]) + bf_ref[...])
        lf_ref[...] = lf

        @pl.when((i % tps) == 0)
        def _():
            carry_ref[...] = jnp.zeros(carry_ref.shape, F32)

        c = _dot3_left(tri_ref[...], lf) + carry_ref[...]
        carry_ref[...] = c[tm - 1:tm, :]
        if transposed:
            for piece, val in enumerate(_split3(-LOG2_E * c)):
                cp_ref[piece] = val
        else:
            c_ref[...] = c

    xkv = xkv_ref[...]
    kk = _dot(xkv, wk_ref[...])
    vv = _dot(xkv, wv_ref[...])
    qq = _dot(xq_ref[...], wq_ref[...])
    v_ref[...] = vv
    scale = HEAD_DIM ** -0.5 * (LOG2_E if transposed else 1.0)
    qn_heads = []
    for c0 in range(0, tn, HEAD_DIM):
        sl = slice(c0, c0 + HEAD_DIM)
        kn = _rms_rows(kk[:, sl]) * gk_ref[...]
        k_ref[:, sl] = kn
        kb_ref[:, sl] = kn.astype(BF16)
        qn_heads.append(_rms_rows(qq[:, sl]) * gq_ref[...] * scale)
    qn = jnp.concatenate(qn_heads, axis=1)
    if transposed:
        vt_ref[0] = vv.T.astype(BF16)
        qt_ref[0] = qn.T.astype(BF16)
        kx = _dot(cp_ref[0], place_ref[0])
        for piece in range(1, N_SPLIT):
            kx = kx + _dot(cp_ref[piece], place_ref[piece])
        kx_ref[...] = kx.astype(BF16)
    else:
        vb_ref[...] = vv.astype(BF16)
        q_ref[...] = qn.astype(BF16)


def _qkv_call(h, g_kv, g_mix, w_k, w_v, w_q, g_k, g_q, w_f_pad, b_f_pad,
              tri, place, tm, tn, seq_len, transposed):
    m, d = h.shape
    tps = max(seq_len // tm, 1)
    kern = functools.partial(_qkv_kernel, tm=tm, tn=tn, tps=tps, transposed=transposed)
    row = lambda i, j: (i, 0)
    const = lambda i, j: (0, 0)
    col = lambda i, j: (0, j)
    tile = lambda i, j: (i, j)
    tile_t = lambda i, j: (i, j, 0)
    in_specs = [
        pl.BlockSpec((tm, d), row),
        pl.BlockSpec((1, d), const),
        pl.BlockSpec((1, d), const),
        pl.BlockSpec((d, tn), col),
        pl.BlockSpec((d, tn), col),
        pl.BlockSpec((d, tn), col),
        pl.BlockSpec((1, HEAD_DIM), const),
        pl.BlockSpec((1, HEAD_DIM), const),
        pl.BlockSpec((d, LANES), const),
        pl.BlockSpec((1, LANES), const),
        pl.BlockSpec((tm, tm), const),
        pl.BlockSpec((N_SPLIT, LANES, tn), lambda i, j: (0, 0, j)),
    ]
    scratch = [pltpu.VMEM((tm, d), BF16), pltpu.VMEM((tm, d), BF16),
               pltpu.VMEM((1, LANES), F32)]
    f32_md = jax.ShapeDtypeStruct((m, d), F32)
    bf_md = jax.ShapeDtypeStruct((m, d), BF16)
    if transposed:
        out_specs = [
            pl.BlockSpec((tm, tn), tile), pl.BlockSpec((tm, tn), tile),
            pl.BlockSpec((tm, tn), tile), pl.BlockSpec((tm, tn), tile),
            pl.BlockSpec((1, tn, tm), tile_t), pl.BlockSpec((1, tn, tm), tile_t),
            pl.BlockSpec((tm, LANES), row),
        ]
        bf_t = jax.ShapeDtypeStruct((m // tm, d, tm), BF16)
        out_shape = [f32_md, f32_md, bf_md, bf_md, bf_t, bf_t,
                     jax.ShapeDtypeStruct((m, LANES), F32)]
        scratch.append(pltpu.VMEM((N_SPLIT, tm, LANES), BF16))
    else:
        out_specs = [
            pl.BlockSpec((tm, tn), tile), pl.BlockSpec((tm, tn), tile),
            pl.BlockSpec((tm, tn), tile), pl.BlockSpec((tm, tn), tile),
            pl.BlockSpec((tm, tn), tile),
            pl.BlockSpec((tm, LANES), row), pl.BlockSpec((tm, LANES), row),
        ]
        out_shape = [f32_md, f32_md, bf_md, bf_md, bf_md,
                     jax.ShapeDtypeStruct((m, LANES), F32),
                     jax.ShapeDtypeStruct((m, LANES), F32)]
    return pl.pallas_call(
        kern,
        grid=(m // tm, d // tn),
        in_specs=in_specs,
        out_specs=out_specs,
        out_shape=out_shape,
        scratch_shapes=scratch,
        compiler_params=_params(2),
        name="qkv_t" if transposed else "qkv",
    )(h, g_kv, g_mix, w_k, w_v, w_q, g_k, g_q, w_f_pad, b_f_pad, tri, place)


def _attn_kernel(q_ref, k_ref, kx_ref, v_ref, o_ref,
                 s0_ref, s1_ref, bm0_ref, bm1_ref, m_ref, acc_ref, *, tq, nb):
    ones_rows = (lax.broadcasted_iota(jnp.int32, (HEAD_DIM, tq), 0) < N_SPLIT).astype(BF16)
    sum_rows = jnp.ones((acc_ref.shape[0] - HEAD_DIM, tq), BF16)
    s_refs = (s0_ref, s1_ref)
    bm_refs = (bm0_ref, bm1_ref)

    def scores(qi, kb, slot):
        q_aug = jnp.concatenate([q_ref[qi], ones_rows], axis=0)
        rows = slice(kb * tq, (kb + 1) * tq)
        k_aug = jnp.concatenate([k_ref[rows, :], kx_ref[rows, :]], axis=1)
        s = _dot(k_aug, q_aug)
        if kb == qi:
            key = lax.broadcasted_iota(jnp.int32, (tq, tq), 0)
            qry = lax.broadcasted_iota(jnp.int32, (tq, tq), 1)
            s = jnp.where(key <= qry, s, NEG_INF)
        s_refs[slot][...] = s
        bm_refs[slot][...] = jnp.max(s, axis=0, keepdims=True)

    def accumulate(qi, kb, slot):
        v_aug = jnp.concatenate([v_ref[kb], sum_rows], axis=0)
        if kb == 0:
            m_new = bm_refs[slot][...]
            p = jnp.exp2((s_refs[slot][...] - m_new).astype(BF16))
            acc_ref[...] = _dot(v_aug, p)
        else:
            m_old = m_ref[...]
            m_new = jnp.maximum(m_old, bm_refs[slot][...])
            alpha = jnp.exp2(m_old - m_new)
            p = jnp.exp2((s_refs[slot][...] - m_new).astype(BF16))
            acc_ref[...] = alpha * acc_ref[...] + _dot(v_aug, p)
        m_ref[...] = m_new
        if kb == qi:
            out = acc_ref[0:HEAD_DIM, :] / acc_ref[HEAD_DIM:HEAD_DIM + 1, :]
            o_ref[qi * tq:(qi + 1) * tq, :] = out.T.astype(o_ref.dtype)

    pairs = [(qi, kb) for qi in range(nb) for kb in range(qi + 1)]
    scores(*pairs[0], 0)
    for i, (qi, kb) in enumerate(pairs):
        if i + 1 < len(pairs):
            scores(*pairs[i + 1], (i + 1) % 2)
        accumulate(qi, kb, i % 2)


def _attn_call(q_t, k, kx, v_t, n_seq, seq_len, tq):
    d = k.shape[1]
    nh = d // HEAD_DIM
    nb = seq_len // tq
    kern = functools.partial(_attn_kernel, tq=tq, nb=nb)
    return pl.pallas_call(
        kern,
        grid=(n_seq, nh),
        in_specs=[
            pl.BlockSpec((nb, HEAD_DIM, tq), lambda n, h: (n, h, 0)),
            pl.BlockSpec((seq_len, HEAD_DIM), lambda n, h: (n, h)),
            pl.BlockSpec((seq_len, HEAD_DIM), lambda n, h: (n, h)),
            pl.BlockSpec((nb, HEAD_DIM, tq), lambda n, h: (n, h, 0)),
        ],
        out_specs=pl.BlockSpec((seq_len, HEAD_DIM), lambda n, h: (n, h)),
        out_shape=jax.ShapeDtypeStruct((n_seq * seq_len, d), BF16),
        scratch_shapes=[pltpu.VMEM((tq, tq), F32), pltpu.VMEM((tq, tq), F32),
                        pltpu.VMEM((1, tq), F32), pltpu.VMEM((1, tq), F32),
                        pltpu.VMEM((1, tq), F32),
                        pltpu.VMEM((HEAD_DIM + 2 * SUBLANES, tq), F32)],
        compiler_params=_params(2),
        name="fox_prompt",
    )(q_t, k, kx, v_t)


def _bias_kernel(pt_ref, *refs, n_steps):
    nb = BIAS_PAGES_PER_STEP
    lf_refs = refs[0:nb]
    sfx_ref, tot_ref, later_ref, spread_ref, o_ref, x_ref = refs[nb:]
    s = pl.program_id(1)
    base = pl.multiple_of(s * nb, nb)
    for e in range(nb):
        x_ref[pl.ds(base + e, 1), :] = lf_refs[e][0]

    @pl.when(s == n_steps - 1)
    def _():
        x = x_ref[...]
        within = _dot3(x, sfx_ref[...])
        tot = _dot3(x, tot_ref[...])
        later = _dot3_left(later_ref[...], tot)
        o_ref[0] = within + _dot3(later, spread_ref[...])


def _bias_call(page_table, lf_flat, sfx, tot_m, later_m, spread):
    b, n_pages = page_table.shape
    width = lf_flat.shape[2]
    nb = BIAS_PAGES_PER_STEP
    n_steps = n_pages // nb

    def page_map(e):
        return lambda bi, s, pt: (pt[bi, s * nb + e], 0, 0)

    const2 = lambda bi, s, pt: (0, 0)
    in_specs = [pl.BlockSpec((1, 1, width), page_map(e)) for e in range(nb)]
    in_specs += [pl.BlockSpec(sfx.shape, const2), pl.BlockSpec(tot_m.shape, const2),
                 pl.BlockSpec(later_m.shape, const2), pl.BlockSpec(spread.shape, const2)]
    grid_spec = pltpu.PrefetchScalarGridSpec(
        num_scalar_prefetch=1,
        grid=(b, n_steps),
        in_specs=in_specs,
        out_specs=pl.BlockSpec((1, n_pages, width), lambda bi, s, pt: (bi, 0, 0)),
        scratch_shapes=[pltpu.VMEM((n_pages, width), F32)],
    )
    return pl.pallas_call(
        functools.partial(_bias_kernel, n_steps=n_steps),
        grid_spec=grid_spec,
        out_shape=jax.ShapeDtypeStruct((b, n_pages, width), F32),
        compiler_params=_params(2),
        name="decode_bias",
    )(page_table, *([lf_flat] * nb), sfx, tot_m, later_m, spread)


def _sattn_kernel(pt_ref, q_ref, *refs, n_steps, n_grp):
    g_pages = PAGES_PER_STEP
    k_refs = refs[0:g_pages]
    v_refs = refs[g_pages:2 * g_pages]
    (bias_ref, mask_ref, kn_ref, vn_ref, bn_ref, maskn_ref, o_ref,
     m_ref, l_ref, acc_ref) = refs[2 * g_pages:]
    g = pl.program_id(1)
    keys_g = mask_ref.shape[1]
    dh = q_ref.shape[3]

    @pl.when(g == 0)
    def _():
        m_ref[...] = jnp.full(m_ref.shape, NEG_INF, F32)
        l_ref[...] = jnp.zeros(l_ref.shape, F32)
        acc_ref[...] = jnp.zeros(acc_ref.shape, F32)

    def update(gi, s, v_rows):
        m_old = m_ref[gi]
        m_new = jnp.maximum(m_old, jnp.max(s, axis=-1, keepdims=True))
        alpha = jnp.exp(m_old - m_new)
        p = jnp.exp(s - m_new)
        l_ref[gi] = alpha * l_ref[gi] + jnp.sum(p, axis=-1, keepdims=True)
        acc_ref[gi] = alpha * acc_ref[gi] + _dot(p.astype(BF16), v_rows)
        m_ref[gi] = m_new

    def group_rows(page_ref, gi):
        tiles = page_ref[0, pl.ds(gi, PAGE_SIZE, stride=n_grp), :, :]
        return tiles.reshape(keys_g, dh).astype(BF16)

    steps_per_block = SUBLANES // g_pages
    bias_rows = [bias_ref[0, pl.ds((g % steps_per_block) * g_pages + e, 1), :]
                 for e in range(g_pages)]

    def group_scores(gi):
        parts = [_dot_nt(q_ref[0, gi], group_rows(k_refs[e], gi))
                 + (mask_ref[...] + bias_rows[e][:, gi * keys_g:(gi + 1) * keys_g])
                 for e in range(g_pages)]
        return jnp.concatenate(parts, axis=1)

    scores = [group_scores(gi) for gi in range(n_grp)]
    for gi in range(n_grp):
        values = jnp.concatenate([group_rows(v_refs[e], gi) for e in range(g_pages)], axis=0)
        update(gi, scores[gi], values)

    @pl.when(g == n_steps - 1)
    def _():
        for gi in range(n_grp):
            s = _dot_nt(q_ref[0, gi], kn_ref[0, gi]) + (maskn_ref[...] + bn_ref[0, gi])
            update(gi, s, vn_ref[0, gi])
        o_ref[0] = (acc_ref[...] / l_ref[...]).astype(o_ref.dtype)


def _sattn_call(page_table, q4, cache_k, cache_v, bias, mask, k_new, v_new,
                bias_new, mask_new):
    b, n_grp, rows, dh = q4.shape
    n_pages = page_table.shape[1]
    n_steps = n_pages // PAGES_PER_STEP
    page_tiles = cache_k.shape[1]
    width = bias.shape[2]
    steps_per_block = SUBLANES // PAGES_PER_STEP

    def page_map(e):
        return lambda bi, g, pt: (pt[bi, g * PAGES_PER_STEP + e], 0, 0, 0)

    per_seq = lambda bi, g, pt: (bi, 0, 0, 0)
    const2 = lambda bi, g, pt: (0, 0)
    page_spec = lambda e: pl.BlockSpec((1, page_tiles, SUBLANES, dh), page_map(e))
    in_specs = [pl.BlockSpec((1, n_grp, rows, dh), per_seq)]
    in_specs += [page_spec(e) for e in range(PAGES_PER_STEP)]
    in_specs += [page_spec(e) for e in range(PAGES_PER_STEP)]
    in_specs += [
        pl.BlockSpec((1, SUBLANES, width), lambda bi, g, pt: (bi, g // steps_per_block, 0)),
        pl.BlockSpec(mask.shape, const2),
        pl.BlockSpec((1, n_grp, rows, dh), per_seq),
        pl.BlockSpec((1, n_grp, rows, dh), per_seq),
        pl.BlockSpec((1, n_grp, 1, rows), per_seq),
        pl.BlockSpec(mask_new.shape, const2),
    ]
    grid_spec = pltpu.PrefetchScalarGridSpec(
        num_scalar_prefetch=1,
        grid=(b, n_steps),
        in_specs=in_specs,
        out_specs=pl.BlockSpec((1, n_grp, rows, dh), per_seq),
        scratch_shapes=[pltpu.VMEM((n_grp, rows, 1), F32), pltpu.VMEM((n_grp, rows, 1), F32),
                        pltpu.VMEM((n_grp, rows, dh), F32)],
    )
    kern = functools.partial(_sattn_kernel, n_steps=n_steps, n_grp=n_grp)
    return pl.pallas_call(
        kern,
        grid_spec=grid_spec,
        out_shape=jax.ShapeDtypeStruct((b, n_grp, rows, dh), BF16),
        compiler_params=_params(2),
        name="fox_decode",
    )(page_table, q4, *([cache_k] * PAGES_PER_STEP), *([cache_v] * PAGES_PER_STEP),
      bias, mask, k_new, v_new, bias_new, mask_new)


def _oproj_kernel(a_ref, w_ref, h_ref, o_ref):
    o_ref[...] = h_ref[...] + _dot(a_ref[...], w_ref[...])


def _oproj_call(a, w_o, h, tm, tn):
    m, d = h.shape
    return pl.pallas_call(
        _oproj_kernel,
        grid=(m // tm, d // tn),
        in_specs=[
            pl.BlockSpec((tm, d), lambda i, j: (i, 0)),
            pl.BlockSpec((d, tn), lambda i, j: (0, j)),
            pl.BlockSpec((tm, tn), lambda i, j: (i, j)),
        ],
        out_specs=pl.BlockSpec((tm, tn), lambda i, j: (i, j)),
        out_shape=jax.ShapeDtypeStruct((m, d), F32),
        compiler_params=_params(2),
        name="o_proj",
    )(a, w_o, h)


def _cumsum_matrix(n, group):
    t = np.arange(n)[:, None]
    tp = np.arange(n)[None, :]
    return ((tp <= t) & (tp // group == t // group)).astype(BF16)


def _placement(nh, d):
    lane = np.arange(LANES)[None, :, None]
    col = np.arange(d)[None, None, :]
    piece = np.arange(N_SPLIT)[:, None, None]
    return ((lane < nh) & (col == lane * HEAD_DIM + piece)).astype(BF16)


def kernel(x_prompt, x_sample, state_conv_a, state_ffn, cache_k, cache_v, cache_logf, page_table, p_prompt, p_sample, g_mix_a, w_pw1, w_dwa, b_dwa, ln_g, ln_b, w_pw2, g_kv, w_k, w_v, g_k, w_f, b_f, g_mix_b, w_q, g_q, w_o, g_ffn, w_up, w_gate, w_fdw, b_fdw, w_down, g_ple, w_ple_gate, w_ple_proj):
    n_p, t_p, d = x_prompt.shape
    n_s, t_s, _ = x_sample.shape
    depth = g_ffn.shape[0]
    n_a = g_mix_a.shape[0]
    f = w_up.shape[2]
    nh = d // HEAD_DIM
    m_s = n_s * t_s

    bf = lambda w: w.astype(BF16)
    layers = lambda w: [bf(w[i]) for i in range(w.shape[0])]
    w_pw1_b, w_pw2_b = layers(w_pw1), layers(w_pw2)
    w_k_b, w_v_b, w_q_b, w_o_b = bf(w_k), bf(w_v), layers(w_q), layers(w_o)
    w_up_b, w_gate_b, w_down_b = bf(w_up), bf(w_gate), bf(w_down)
    w_pg_b, w_pp_b = bf(w_ple_gate), bf(w_ple_proj)
    w_f_pad = jnp.pad(bf(w_f), ((0, 0), (0, LANES - nh)))
    b_f_pad = jnp.pad(b_f, (0, LANES - nh)).reshape(1, LANES)
    place = _placement(nh, d)
    row = lambda a: a.reshape(1, -1)

    def trunk(x, p, n_seq, seq_len, a_prev32, tiles, ffn, attend, tri, transposed):
        h = x.reshape(n_seq * seq_len, d)
        a_us, f_bufs = [], []
        k = v = lf = None
        for i in range(depth):
            attn = None
            if i < n_a:
                u = _glu_call(h, row(g_mix_a[i]), w_pw1_b[i], tiles.rows, tiles.cols)
                h = _conva_call(u, a_prev32[i], w_dwa[i], row(b_dwa[i]), row(ln_g[i]),
                                row(ln_b[i]), w_pw2_b[i], h, tiles.rows_conv, seq_len)
                a_us.append(u.reshape(n_seq, seq_len, d))
            else:
                jb = i - n_a
                outs = _qkv_call(
                    h, row(g_kv), row(g_mix_b[jb]), w_k_b, w_v_b, w_q_b[jb],
                    row(g_k), row(g_q[jb]), w_f_pad, b_f_pad, tri, place,
                    tiles.rows_qkv, tiles.cols, seq_len, transposed)
                k, v = outs[0], outs[1]
                lf = outs[6] if transposed else outs[5]
                attn = (attend(outs), w_o_b[jb])
            h, fb = ffn(i, h, attn)
            f_bufs.append(fb)
            h = _ple_call(h, row(g_ple[i]), p[i].reshape(n_seq * seq_len, -1),
                          w_pg_b, w_pp_b, tiles.rows_ple, i)
        shp = (n_seq, seq_len, nh, HEAD_DIM)
        return (h.reshape(n_seq, seq_len, d), a_us, jnp.stack(f_bufs),
                k.reshape(shp), v.reshape(shp),
                lf[:, :nh].reshape(n_seq, seq_len, nh))

    tp = PROMPT_TILES

    def ffn_prompt(i, h, attn):
        prev = jnp.zeros((n_p, CONV_F_WIDTH - 1, f), F32)
        return _ffn_seq_call(h, row(g_ffn[i]), w_up_b, w_gate_b, w_fdw[i],
                             row(b_fdw[i]), w_down_b, prev, tp.rows_ffn, tp.cols, t_p, i,
                             attn)

    def attend_prompt(outs):
        _, _, kb, kx, v_t, q_t, _ = outs
        return _attn_call(q_t, kb, kx, v_t, n_p, t_p, tp.rows_qkv)

    a_prev_p = [jnp.zeros((n_p, HALO_ROWS, d), F32) for _ in range(n_a)]
    (y_p, a_us_p, ffn_p, k_p, v_p, lf_p) = trunk(
        x_prompt, p_prompt, n_p, t_p, a_prev_p, tp, ffn_prompt,
        attend_prompt, _cumsum_matrix(tp.rows_qkv, tp.rows_qkv), True)
    conv_a_p = jnp.stack([u[:, t_p - (CONV_A_WIDTH - 1):] for u in a_us_p])

    def ffn_sample(i, h, attn):
        if attn is not None:
            h = _oproj_call(attn[0], attn[1], h, m_s, tp.cols)
        st = state_ffn[i]
        z = jnp.zeros((n_s, t_s, f), F32)
        p1 = z.at[:, 0].set(st[:, 1]).reshape(m_s, f)
        p2 = z.at[:, 0].set(st[:, 0]).at[:, 1].set(st[:, 1]).reshape(m_s, f)
        h, u = _ffn_multi_call(h, row(g_ffn[i]), w_up_b, w_gate_b, w_fdw[i],
                               row(b_fdw[i]), w_down_b, p1, p2, tp.cols, t_s, i)
        full = jnp.concatenate([st, u.reshape(n_s, t_s, f)], axis=1)
        return h, full[:, -(CONV_F_WIDTH - 1):]

    n_pool = cache_k.shape[0]
    n_pages = page_table.shape[1]
    n_grp = nh // SUBLANES
    keys_g = PAGE_SIZE * SUBLANES
    page_rows = PAGE_SIZE * nh
    tiles_shape = (n_pool, PAGE_SIZE * n_grp, SUBLANES, HEAD_DIM)
    cache_k4 = cache_k.reshape(tiles_shape)
    cache_v4 = cache_v.reshape(tiles_shape)
    lf_flat = cache_logf.reshape(n_pool, 1, page_rows)
    fr = np.arange(page_rows)
    t_in, h_in = fr // nh, fr % nh
    t_out = (fr % keys_g) // SUBLANES
    h_out = (fr // keys_g) * SUBLANES + fr % SUBLANES
    lane = np.arange(LANES)
    sfx = ((h_in[:, None] == h_out[None, :]) & (t_in[:, None] > t_out[None, :])).astype(BF16)
    tot_m = (h_in[:, None] == lane[None, :]).astype(BF16)
    pg = np.arange(n_pages)
    later_m = (pg[None, :] > pg[:, None]).astype(BF16)
    spread = (lane[:, None] == h_out[None, :]).astype(BF16)
    qrow = np.arange(t_s * SUBLANES)
    kcol = np.arange(keys_g)
    mask = np.where((qrow[:, None] % SUBLANES) == (kcol[None, :] % SUBLANES),
                    0.0, NEG_INF).astype(np.float32)
    mask_new = np.where(((qrow[:, None] % SUBLANES) == (qrow[None, :] % SUBLANES))
                        & (qrow[None, :] // SUBLANES <= qrow[:, None] // SUBLANES),
                        0.0, NEG_INF).astype(np.float32)
    bias = _bias_call(page_table, lf_flat, sfx, tot_m, later_m, spread)

    def by_group(a, width):
        a5 = a.reshape(n_s, t_s, n_grp, SUBLANES, width)
        return jnp.transpose(a5, (0, 2, 1, 3, 4)).reshape(n_s, n_grp, t_s * SUBLANES, width)

    def attend_sample(outs):
        _, _, kb, vb, q, _, c = outs
        bias_new = -by_group(c[:, :nh], 1).reshape(n_s, n_grp, 1, t_s * SUBLANES)
        a = _sattn_call(page_table, by_group(q, HEAD_DIM), cache_k4, cache_v4, bias, mask,
                        by_group(kb, HEAD_DIM), by_group(vb, HEAD_DIM), bias_new, mask_new)
        a5 = a.reshape(n_s, n_grp, t_s, SUBLANES, HEAD_DIM)
        return jnp.transpose(a5, (0, 2, 1, 3, 4)).reshape(m_s, d)

    a_prev_s = [jnp.pad(state_conv_a[i], ((0, 0), (HALO_ROWS - (CONV_A_WIDTH - 1), 0), (0, 0)))
                for i in range(n_a)]
    ts = _Tiles(rows=m_s, rows_qkv=m_s, rows_ffn=m_s, rows_conv=t_s, rows_ple=m_s,
                cols=tp.cols)
    (y_s, a_us_s, ffn_s, k_s, v_s, lf_s) = trunk(
        x_sample, p_sample, n_s, t_s, a_prev_s, ts, ffn_sample,
        attend_sample, _cumsum_matrix(m_s, t_s), False)
    conv_a_s = jnp.stack([
        jnp.concatenate([state_conv_a[i], a_us_s[i]], axis=1)[:, -(CONV_A_WIDTH - 1):]
        for i in range(n_a)])

    return (y_p, y_s, conv_a_p, ffn_p, k_p, v_p, lf_p,
            conv_a_s, ffn_s, k_s, v_s, lf_s)
```

```python
import functools
from typing import NamedTuple

import jax
import jax.numpy as jnp
import numpy as np
from jax import lax
from jax.experimental import pallas as pl
from jax.experimental.pallas import tpu as pltpu

F32 = jnp.float32
BF16 = jnp.bfloat16

HEAD_DIM = 128
CONV_A_WIDTH = 31
CONV_F_WIDTH = 3
PAGE_SIZE = 128
EPS = 1e-6
NEG_INF = -1e30
LOG2_E = 1.4426950408889634

LANES = 128
SUBLANES = 8
HALO_ROWS = 32
VMEM_LIMIT = 56 * 1024 * 1024
PAGES_PER_STEP = 8
BIAS_PAGES_PER_STEP = 32
N_SPLIT = 3
QUERY_GROUP = 256


class _Tiles(NamedTuple):
    rows: int
    rows_qkv: int
    rows_ffn: int
    rows_conv: int
    rows_ple: int
    cols: int


PROMPT_TILES = _Tiles(rows=1024, rows_qkv=512, rows_ffn=512, rows_conv=256,
                      rows_ple=512, cols=512)


def _params(n_axes):
    return pltpu.CompilerParams(
        dimension_semantics=("arbitrary",) * n_axes,
        vmem_limit_bytes=VMEM_LIMIT)


def _dot(a, b):
    return jnp.dot(a, b, preferred_element_type=F32)


def _dot_nt(a, b):
    return lax.dot_general(a, b, (((1,), (1,)), ((), ())),
                           preferred_element_type=F32)


def _sigmoid(x):
    return jax.nn.sigmoid(x)


def _log_sigmoid(z):
    return -(jnp.maximum(-z, 0.0) + jnp.log1p(jnp.exp(-jnp.abs(z))))


def _rms_rows(x):
    ms = jnp.mean(x * x, axis=-1, keepdims=True)
    return x * lax.rsqrt(ms + EPS)


def _split3(x):
    hi = x.astype(BF16)
    r1 = x - hi.astype(F32)
    mid = r1.astype(BF16)
    lo = (r1 - mid.astype(F32)).astype(BF16)
    return hi, mid, lo


def _dot3(x, w_bf16):
    hi, mid, lo = _split3(x)
    return _dot(hi, w_bf16) + _dot(mid, w_bf16) + _dot(lo, w_bf16)


def _dot3_left(w_bf16, x):
    hi, mid, lo = _split3(x)
    return _dot(w_bf16, hi) + _dot(w_bf16, mid) + _dot(w_bf16, lo)


def _glu_kernel(x_ref, g_ref, w1_ref, w2_ref, u_ref, xn_ref):
    @pl.when(pl.program_id(1) == 0)
    def _():
        xn_ref[...] = (_rms_rows(x_ref[...]) * g_ref[...]).astype(BF16)

    xn = xn_ref[...]
    a1 = _dot(xn, w1_ref[...])
    a2 = _dot(xn, w2_ref[...])
    u_ref[...] = a1 * _sigmoid(a2)


def _glu_call(h, g, w_pw1, tm, tn):
    m, d = h.shape
    nj = d // tn
    return pl.pallas_call(
        _glu_kernel,
        grid=(m // tm, nj),
        in_specs=[
            pl.BlockSpec((tm, d), lambda i, j: (i, 0)),
            pl.BlockSpec((1, d), lambda i, j: (0, 0)),
            pl.BlockSpec((d, tn), lambda i, j: (0, j)),
            pl.BlockSpec((d, tn), lambda i, j: (0, j + nj)),
        ],
        out_specs=pl.BlockSpec((tm, tn), lambda i, j: (i, j)),
        out_shape=jax.ShapeDtypeStruct((m, d), F32),
        scratch_shapes=[pltpu.VMEM((tm, d), BF16)],
        compiler_params=_params(2),
        name="glu",
    )(h, g, w_pw1, w_pw1)


def _conva_kernel(u_ref, halo_ref, prev_ref, wdw_ref, bdw_ref, lng_ref, lnb_ref,
                  w2_ref, h_ref, o_ref, full_ref, c_ref, *, tm, tps, rb):
    i = pl.program_id(0)
    d = u_ref.shape[1]
    first = (i % tps) == 0
    full_ref[0:HALO_ROWS, :] = jnp.where(first, prev_ref[0], halo_ref[...])
    full_ref[HALO_ROWS:HALO_ROWS + tm, :] = u_ref[...]
    full_ref[HALO_ROWS + tm:HALO_ROWS + tm + SUBLANES, :] = jnp.zeros((SUBLANES, d), F32)
    off = HALO_ROWS - (CONV_A_WIDTH - 1)

    def conv_chunk(r0, c0):
        cols = slice(c0, c0 + LANES)
        acc = jnp.broadcast_to(bdw_ref[:, cols], (rb, LANES))
        for b in range(SUBLANES):
            z = None
            for j in range(CONV_A_WIDTH):
                if (off + j) % SUBLANES != b:
                    continue
                s = r0 + (off + j) // SUBLANES * SUBLANES
                term = wdw_ref[j:j + 1, cols] * full_ref[s:s + rb + SUBLANES, cols]
                z = term if z is None else z + term
            acc = acc + z[b:b + rb, :]
        c_ref[r0:r0 + rb, cols] = acc

    for r0 in range(0, tm, rb):
        for c0 in range(0, d, LANES):
            conv_chunk(r0, c0)
    c = c_ref[...]
    mu = jnp.mean(c, axis=-1, keepdims=True)
    xc = c - mu
    var = jnp.mean(xc * xc, axis=-1, keepdims=True)
    y = xc * lax.rsqrt(var + EPS) * lng_ref[...] + lnb_ref[...]
    a = (y * _sigmoid(y)).astype(BF16)
    o_ref[...] = h_ref[...] + _dot(a, w2_ref[...])


def _conva_call(u, prev32, w_dw, b_dw, ln_g, ln_b, w_pw2, h, tm, seq_len):
    m, d = u.shape
    tps = max(seq_len // tm, 1)
    rb = min(tm, 128)
    if tm % HALO_ROWS == 0:
        k = tm // HALO_ROWS
        halo_map = lambda i: (jnp.maximum(i * k - 1, 0), 0)
    else:
        assert tps == 1
        halo_map = lambda i: (0, 0)
    kern = functools.partial(_conva_kernel, tm=tm, tps=tps, rb=rb)
    return pl.pallas_call(
        kern,
        grid=(m // tm,),
        in_specs=[
            pl.BlockSpec((tm, d), lambda i: (i, 0)),
            pl.BlockSpec((HALO_ROWS, d), halo_map),
            pl.BlockSpec((1, HALO_ROWS, d), lambda i: (i // tps, 0, 0)),
            pl.BlockSpec((CONV_A_WIDTH, d), lambda i: (0, 0)),
            pl.BlockSpec((1, d), lambda i: (0, 0)),
            pl.BlockSpec((1, d), lambda i: (0, 0)),
            pl.BlockSpec((1, d), lambda i: (0, 0)),
            pl.BlockSpec((d, d), lambda i: (0, 0), pipeline_mode=pl.Buffered(1)),
            pl.BlockSpec((tm, d), lambda i: (i, 0)),
        ],
        out_specs=pl.BlockSpec((tm, d), lambda i: (i, 0)),
        out_shape=jax.ShapeDtypeStruct((m, d), F32),
        scratch_shapes=[pltpu.VMEM((HALO_ROWS + tm + SUBLANES, d), F32),
                        pltpu.VMEM((tm, d), F32)],
        compiler_params=_params(1),
        name="conv_a",
    )(u, u, prev32, w_dw, b_dw, ln_g, ln_b, w_pw2, h)


def _ffn_body(h_ref, g_ref, wu_ref, wg_ref, wdw_ref, bdw_ref, wd_ref,
              o_ref, xn_ref, ubuf_ref, tm, prev_rows, fix_shifted):
    xn = xn_ref[...]
    u = _dot(xn, wu_ref[...])
    ubuf_ref[SUBLANES:SUBLANES + tm, :] = u
    ubuf_ref[SUBLANES - 2:SUBLANES, :] = prev_rows
    u1 = ubuf_ref[SUBLANES - 1:SUBLANES - 1 + tm, :]
    u2 = ubuf_ref[SUBLANES - 2:SUBLANES - 2 + tm, :]
    u1, u2 = fix_shifted(u1, u2)
    c = (wdw_ref[0:1, :] * u2 + wdw_ref[1:2, :] * u1 + wdw_ref[2:3, :] * u
         + bdw_ref[...])
    gt = _dot(xn, wg_ref[...])
    act = (c * _sigmoid(c) * gt).astype(BF16)
    o_ref[...] += _dot(act, wd_ref[...])


def _ffn_seq_kernel(h_ref, g_ref, wu_ref, wg_ref, wdw_ref, bdw_ref, wd_ref,
                    *rest, tm, tps, with_attn):
    if with_attn:
        a_ref, wo_ref, *rest = rest
    prev_ref, o_ref, fb_ref, xn_ref, ubuf_ref, carry_ref = rest
    i = pl.program_id(0)
    j = pl.program_id(1)
    first = (i % tps) == 0

    @pl.when(j == 0)
    def _():
        h = h_ref[...]
        if with_attn:
            h = h + _dot(a_ref[...], wo_ref[...])
        xn_ref[...] = (_rms_rows(h) * g_ref[...]).astype(BF16)
        o_ref[...] = h

        @pl.when(i == 0)
        def _():
            carry_ref[...] = jnp.zeros(carry_ref.shape, F32)

    prev_rows = jnp.where(first, prev_ref[0], carry_ref[j])
    _ffn_body(h_ref, g_ref, wu_ref, wg_ref, wdw_ref, bdw_ref, wd_ref,
              o_ref, xn_ref, ubuf_ref, tm, prev_rows, lambda a, b: (a, b))
    tail = ubuf_ref[SUBLANES + tm - 2:SUBLANES + tm, :]
    carry_ref[j] = tail

    @pl.when((i % tps) == tps - 1)
    def _():
        fb_ref[i // tps, j] = tail


def _ffn_multi_kernel(h_ref, g_ref, wu_ref, wg_ref, wdw_ref, bdw_ref, wd_ref,
                      p1_ref, p2_ref, o_ref, u_ref, xn_ref, ubuf_ref,
                      *, tm, seq_len):
    @pl.when(pl.program_id(1) == 0)
    def _():
        xn_ref[...] = (_rms_rows(h_ref[...]) * g_ref[...]).astype(BF16)
        o_ref[...] = h_ref[...]
        ubuf_ref[0:SUBLANES, :] = jnp.zeros((SUBLANES, ubuf_ref.shape[1]), F32)

    def fix_shifted(u1, u2):
        t = lax.broadcasted_iota(jnp.int32, (tm, 1), 0) % seq_len
        return (jnp.where(t >= 1, u1, p1_ref[...]),
                jnp.where(t >= 2, u2, p2_ref[...]))

    prev_rows = jnp.zeros((CONV_F_WIDTH - 1, ubuf_ref.shape[1]), F32)
    _ffn_body(h_ref, g_ref, wu_ref, wg_ref, wdw_ref, bdw_ref, wd_ref,
              o_ref, xn_ref, ubuf_ref, tm, prev_rows, fix_shifted)
    u_ref[...] = ubuf_ref[SUBLANES:SUBLANES + tm, :]


def _ffn_common_specs(tm, tn, d, f, layer):
    return [
        pl.BlockSpec((tm, d), lambda i, j: (i, 0)),
        pl.BlockSpec((1, d), lambda i, j: (0, 0)),
        pl.BlockSpec((None, d, tn), lambda i, j: (layer, 0, j)),
        pl.BlockSpec((None, d, tn), lambda i, j: (layer, 0, j)),
        pl.BlockSpec((CONV_F_WIDTH, tn), lambda i, j: (0, j)),
        pl.BlockSpec((1, tn), lambda i, j: (0, j)),
        pl.BlockSpec((None, tn, d), lambda i, j: (layer, j, 0)),
    ]


def _ffn_seq_call(h, g, w_up, w_gate, w_dw, b_dw, w_down, prev, tm, tn, seq_len, layer,
                  attn=None):
    m, d = h.shape
    f = w_up.shape[2]
    n_seq = m // seq_len
    tps = seq_len // tm
    nj = f // tn
    kern = functools.partial(_ffn_seq_kernel, tm=tm, tps=tps, with_attn=attn is not None)
    attn_specs = [] if attn is None else [
        pl.BlockSpec((tm, d), lambda i, j: (i, 0)),
        pl.BlockSpec((d, d), lambda i, j: (0, 0), pipeline_mode=pl.Buffered(1)),
    ]
    h_new, fb = pl.pallas_call(
        kern,
        grid=(m // tm, nj),
        in_specs=_ffn_common_specs(tm, tn, d, f, layer) + attn_specs + [
            pl.BlockSpec((1, CONV_F_WIDTH - 1, tn), lambda i, j: (i // tps, 0, j)),
        ],
        out_specs=[
            pl.BlockSpec((tm, d), lambda i, j: (i, 0)),
            pl.BlockSpec((n_seq, nj, CONV_F_WIDTH - 1, tn), lambda i, j: (0, 0, 0, 0)),
        ],
        out_shape=[jax.ShapeDtypeStruct((m, d), F32),
                   jax.ShapeDtypeStruct((n_seq, nj, CONV_F_WIDTH - 1, tn), F32)],
        scratch_shapes=[pltpu.VMEM((tm, d), BF16),
                        pltpu.VMEM((SUBLANES + tm, tn), F32),
                        pltpu.VMEM((nj, CONV_F_WIDTH - 1, tn), F32)],
        compiler_params=_params(2),
        name="ffn_seq",
    )(h, g, w_up, w_gate, w_dw, b_dw, w_down, *(attn or ()), prev)
    fb = jnp.transpose(fb, (0, 2, 1, 3)).reshape(n_seq, CONV_F_WIDTH - 1, f)
    return h_new, fb


def _ffn_multi_call(h, g, w_up, w_gate, w_dw, b_dw, w_down, p1, p2, tn, seq_len, layer):
    m, d = h.shape
    f = w_up.shape[2]
    kern = functools.partial(_ffn_multi_kernel, tm=m, seq_len=seq_len)
    return pl.pallas_call(
        kern,
        grid=(1, f // tn),
        in_specs=_ffn_common_specs(m, tn, d, f, layer) + [
            pl.BlockSpec((m, tn), lambda i, j: (0, j)),
            pl.BlockSpec((m, tn), lambda i, j: (0, j)),
        ],
        out_specs=[
            pl.BlockSpec((m, d), lambda i, j: (0, 0)),
            pl.BlockSpec((m, tn), lambda i, j: (0, j)),
        ],
        out_shape=[jax.ShapeDtypeStruct((m, d), F32),
                   jax.ShapeDtypeStruct((m, f), F32)],
        scratch_shapes=[pltpu.VMEM((m, d), BF16),
                        pltpu.VMEM((SUBLANES + m, tn), F32)],
        compiler_params=_params(2),
        name="ffn_multi",
    )(h, g, w_up, w_gate, w_dw, b_dw, w_down, p1, p2)


def _ple_kernel(h_ref, g_ref, p_ref, wg_ref, wp_ref, o_ref):
    h = h_ref[...]
    xn = (_rms_rows(h) * g_ref[...]).astype(BF16)
    gate = _dot(xn, wg_ref[...])
    proj = _dot(p_ref[...].astype(BF16), wp_ref[...])
    o_ref[...] = h + _sigmoid(gate) * proj


def _ple_call(h, g, p, w_gate, w_proj, tm, layer):
    m, d = h.shape
    pd = p.shape[1]
    resident = pl.Buffered(1)
    return pl.pallas_call(
        _ple_kernel,
        grid=(m // tm,),
        in_specs=[
            pl.BlockSpec((tm, d), lambda i: (i, 0)),
            pl.BlockSpec((1, d), lambda i: (0, 0)),
            pl.BlockSpec((tm, pd), lambda i: (i, 0)),
            pl.BlockSpec((None, d, d), lambda i: (layer, 0, 0), pipeline_mode=resident),
            pl.BlockSpec((None, pd, d), lambda i: (layer, 0, 0), pipeline_mode=resident),
        ],
        out_specs=pl.BlockSpec((tm, d), lambda i: (i, 0)),
        out_shape=jax.ShapeDtypeStruct((m, d), F32),
        compiler_params=_params(1),
        name="ple",
    )(h, g, p, w_gate, w_proj)


def _qkv_kernel(h_ref, gkv_ref, gmix_ref, wk_ref, wv_ref, wq_ref, gk_ref, gq_ref,
                wf_ref, bf_ref, tri_ref, place_ref, *refs, tm, tn, tps, transposed):
    if transposed:
        (k_ref, v_ref, kb_ref, kx_ref, vt_ref, qt_ref, lf_ref,
         xkv_ref, xq_ref, carry_ref, cp_ref) = refs
    else:
        (k_ref, v_ref, kb_ref, vb_ref, q_ref, lf_ref, c_ref,
         xkv_ref, xq_ref, carry_ref) = refs
    i = pl.program_id(0)
    j = pl.program_id(1)

    @pl.when(j == 0)
    def _():
        r = _rms_rows(h_ref[...])
        xkv = (r * gkv_ref[...]).astype(BF16)
        xkv_ref[...] = xkv
        xq_ref[...] = (r * gmix_ref[...]).astype(BF16)
        lf = _log_sigmoid(_dot(xkv, wf_ref[...]) + bf_ref[...])
        lf_ref[...] = lf

        @pl.when((i % tps) == 0)
        def _():
            carry_ref[...] = jnp.zeros(carry_ref.shape, F32)

        c = _dot3_left(tri_ref[...], lf) + carry_ref[...]
        carry_ref[...] = c[tm - 1:tm, :]
        if transposed:
            for piece, val in enumerate(_split3(-LOG2_E * c)):
                cp_ref[piece] = val
        else:
            c_ref[...] = c

    xkv = xkv_ref[...]
    kk = _dot(xkv, wk_ref[...])
    vv = _dot(xkv, wv_ref[...])
    qq = _dot(xq_ref[...], wq_ref[...])
    v_ref[...] = vv
    scale = HEAD_DIM ** -0.5 * (LOG2_E if transposed else 1.0)
    qn_heads = []
    for c0 in range(0, tn, HEAD_DIM):
        sl = slice(c0, c0 + HEAD_DIM)
        kn = _rms_rows(kk[:, sl]) * gk_ref[...]
        k_ref[:, sl] = kn
        kb_ref[:, sl] = kn.astype(BF16)
        qn_heads.append(_rms_rows(qq[:, sl]) * gq_ref[...] * scale)
    qn = jnp.concatenate(qn_heads, axis=1)
    if transposed:
        vt_ref[0] = vv.T.astype(BF16)
        qt_ref[0] = qn.T.astype(BF16)
        kx = _dot(cp_ref[0], place_ref[0])
        for piece in range(1, N_SPLIT):
            kx = kx + _dot(cp_ref[piece], place_ref[piece])
        kx_ref[...] = kx.astype(BF16)
    else:
        vb_ref[...] = vv.astype(BF16)
        q_ref[...] = qn.astype(BF16)


def _qkv_call(h, g_kv, g_mix, w_k, w_v, w_q, g_k, g_q, w_f_pad, b_f_pad,
              tri, place, tm, tn, seq_len, transposed):
    m, d = h.shape
    tps = max(seq_len // tm, 1)
    kern = functools.partial(_qkv_kernel, tm=tm, tn=tn, tps=tps, transposed=transposed)
    row = lambda i, j: (i, 0)
    const = lambda i, j: (0, 0)
    col = lambda i, j: (0, j)
    tile = lambda i, j: (i, j)
    tile_t = lambda i, j: (i, j, 0)
    in_specs = [
        pl.BlockSpec((tm, d), row),
        pl.BlockSpec((1, d), const),
        pl.BlockSpec((1, d), const),
        pl.BlockSpec((d, tn), col),
        pl.BlockSpec((d, tn), col),
        pl.BlockSpec((d, tn), col),
        pl.BlockSpec((1, HEAD_DIM), const),
        pl.BlockSpec((1, HEAD_DIM), const),
        pl.BlockSpec((d, LANES), const),
        pl.BlockSpec((1, LANES), const),
        pl.BlockSpec((tm, tm), const),
        pl.BlockSpec((N_SPLIT, LANES, tn), lambda i, j: (0, 0, j)),
    ]
    scratch = [pltpu.VMEM((tm, d), BF16), pltpu.VMEM((tm, d), BF16),
               pltpu.VMEM((1, LANES), F32)]
    f32_md = jax.ShapeDtypeStruct((m, d), F32)
    bf_md = jax.ShapeDtypeStruct((m, d), BF16)
    if transposed:
        out_specs = [
            pl.BlockSpec((tm, tn), tile), pl.BlockSpec((tm, tn), tile),
            pl.BlockSpec((tm, tn), tile), pl.BlockSpec((tm, tn), tile),
            pl.BlockSpec((1, tn, tm), tile_t), pl.BlockSpec((1, tn, tm), tile_t),
            pl.BlockSpec((tm, LANES), row),
        ]
        bf_t = jax.ShapeDtypeStruct((m // tm, d, tm), BF16)
        out_shape = [f32_md, f32_md, bf_md, bf_md, bf_t, bf_t,
                     jax.ShapeDtypeStruct((m, LANES), F32)]
        scratch.append(pltpu.VMEM((N_SPLIT, tm, LANES), BF16))
    else:
        out_specs = [
            pl.BlockSpec((tm, tn), tile), pl.BlockSpec((tm, tn), tile),
            pl.BlockSpec((tm, tn), tile), pl.BlockSpec((tm, tn), tile),
            pl.BlockSpec((tm, tn), tile),
            pl.BlockSpec((tm, LANES), row), pl.BlockSpec((tm, LANES), row),
        ]
        out_shape = [f32_md, f32_md, bf_md, bf_md, bf_md,
                     jax.ShapeDtypeStruct((m, LANES), F32),
                     jax.ShapeDtypeStruct((m, LANES), F32)]
    return pl.pallas_call(
        kern,
        grid=(m // tm, d // tn),
        in_specs=in_specs,
        out_specs=out_specs,
        out_shape=out_shape,
        scratch_shapes=scratch,
        compiler_params=_params(2),
        name="qkv_t" if transposed else "qkv",
    )(h, g_kv, g_mix, w_k, w_v, w_q, g_k, g_q, w_f_pad, b_f_pad, tri, place)


def _attn_kernel(q_ref, k_ref, kx_ref, v_ref, o_ref,
                 s0_ref, s1_ref, bm0_ref, bm1_ref, m_ref, acc_ref, *, tq, nb):
    ones_rows = (lax.broadcasted_iota(jnp.int32, (HEAD_DIM, tq), 0) < N_SPLIT).astype(BF16)
    sum_rows = jnp.ones((acc_ref.shape[0] - HEAD_DIM, tq), BF16)
    s_refs = (s0_ref, s1_ref)
    bm_refs = (bm0_ref, bm1_ref)

    def scores(qi, kb, slot):
        q_aug = jnp.concatenate([q_ref[qi], ones_rows], axis=0)
        rows = slice(kb * tq, (kb + 1) * tq)
        k_aug = jnp.concatenate([k_ref[rows, :], kx_ref[rows, :]], axis=1)
        for c0 in range(0, tq, QUERY_GROUP):
            cols = slice(c0, c0 + QUERY_GROUP)
            s = _dot(k_aug, q_aug[:, cols])
            if kb == qi:
                key = lax.broadcasted_iota(jnp.int32, s.shape, 0)
                qry = lax.broadcasted_iota(jnp.int32, s.shape, 1) + c0
                s = jnp.where(key <= qry, s, NEG_INF)
            s_refs[slot][:, cols] = s
            bm_refs[slot][:, cols] = jnp.max(s, axis=0, keepdims=True)

    def accumulate(qi, kb, slot):
        v_aug = jnp.concatenate([v_ref[kb], sum_rows], axis=0)
        for c0 in range(0, tq, QUERY_GROUP):
            cols = slice(c0, c0 + QUERY_GROUP)
            if kb == 0:
                m_new = bm_refs[slot][:, cols]
                p = jnp.exp2((s_refs[slot][:, cols] - m_new).astype(BF16))
                acc_ref[:, cols] = _dot(v_aug, p)
            else:
                m_old = m_ref[:, cols]
                m_new = jnp.maximum(m_old, bm_refs[slot][:, cols])
                alpha = jnp.exp2(m_old - m_new)
                p = jnp.exp2((s_refs[slot][:, cols] - m_new).astype(BF16))
                acc_ref[:, cols] = alpha * acc_ref[:, cols] + _dot(v_aug, p)
            m_ref[:, cols] = m_new
        if kb == qi:
            out = acc_ref[0:HEAD_DIM, :] / acc_ref[HEAD_DIM:HEAD_DIM + 1, :]
            o_ref[qi * tq:(qi + 1) * tq, :] = out.T.astype(o_ref.dtype)

    pairs = [(qi, kb) for qi in range(nb) for kb in range(qi + 1)]
    scores(*pairs[0], 0)
    for i, (qi, kb) in enumerate(pairs):
        if i + 1 < len(pairs):
            scores(*pairs[i + 1], (i + 1) % 2)
        accumulate(qi, kb, i % 2)


def _attn_call(q_t, k, kx, v_t, n_seq, seq_len, tq):
    d = k.shape[1]
    nh = d // HEAD_DIM
    nb = seq_len // tq
    kern = functools.partial(_attn_kernel, tq=tq, nb=nb)
    return pl.pallas_call(
        kern,
        grid=(n_seq, nh),
        in_specs=[
            pl.BlockSpec((nb, HEAD_DIM, tq), lambda n, h: (n, h, 0)),
            pl.BlockSpec((seq_len, HEAD_DIM), lambda n, h: (n, h)),
            pl.BlockSpec((seq_len, HEAD_DIM), lambda n, h: (n, h)),
            pl.BlockSpec((nb, HEAD_DIM, tq), lambda n, h: (n, h, 0)),
        ],
        out_specs=pl.BlockSpec((seq_len, HEAD_DIM), lambda n, h: (n, h)),
        out_shape=jax.ShapeDtypeStruct((n_seq * seq_len, d), BF16),
        scratch_shapes=[pltpu.VMEM((tq, tq), F32), pltpu.VMEM((tq, tq), F32),
                        pltpu.VMEM((1, tq), F32), pltpu.VMEM((1, tq), F32),
                        pltpu.VMEM((1, tq), F32),
                        pltpu.VMEM((HEAD_DIM + 2 * SUBLANES, tq), F32)],
        compiler_params=_params(2),
        name="fox_prompt",
    )(q_t, k, kx, v_t)


def _bias_kernel(pt_ref, *refs, n_steps):
    nb = BIAS_PAGES_PER_STEP
    lf_refs = refs[0:nb]
    sfx_ref, tot_ref, later_ref, spread_ref, o_ref, x_ref = refs[nb:]
    s = pl.program_id(1)
    base = pl.multiple_of(s * nb, nb)
    for e in range(nb):
        x_ref[pl.ds(base + e, 1), :] = lf_refs[e][0]

    @pl.when(s == n_steps - 1)
    def _():
        x = x_ref[...]
        within = _dot3(x, sfx_ref[...])
        tot = _dot3(x, tot_ref[...])
        later = _dot3_left(later_ref[...], tot)
        o_ref[0] = within + _dot3(later, spread_ref[...])


def _bias_call(page_table, lf_flat, sfx, tot_m, later_m, spread):
    b, n_pages = page_table.shape
    width = lf_flat.shape[2]
    nb = BIAS_PAGES_PER_STEP
    n_steps = n_pages // nb

    def page_map(e):
        return lambda bi, s, pt: (pt[bi, s * nb + e], 0, 0)

    const2 = lambda bi, s, pt: (0, 0)
    in_specs = [pl.BlockSpec((1, 1, width), page_map(e)) for e in range(nb)]
    in_specs += [pl.BlockSpec(sfx.shape, const2), pl.BlockSpec(tot_m.shape, const2),
                 pl.BlockSpec(later_m.shape, const2), pl.BlockSpec(spread.shape, const2)]
    grid_spec = pltpu.PrefetchScalarGridSpec(
        num_scalar_prefetch=1,
        grid=(b, n_steps),
        in_specs=in_specs,
        out_specs=pl.BlockSpec((1, n_pages, width), lambda bi, s, pt: (bi, 0, 0)),
        scratch_shapes=[pltpu.VMEM((n_pages, width), F32)],
    )
    return pl.pallas_call(
        functools.partial(_bias_kernel, n_steps=n_steps),
        grid_spec=grid_spec,
        out_shape=jax.ShapeDtypeStruct((b, n_pages, width), F32),
        compiler_params=_params(2),
        name="decode_bias",
    )(page_table, *([lf_flat] * nb), sfx, tot_m, later_m, spread)


def _sattn_kernel(pt_ref, q_ref, *refs, n_steps, n_grp):
    g_pages = PAGES_PER_STEP
    k_refs = refs[0:g_pages]
    v_refs = refs[g_pages:2 * g_pages]
    (bias_ref, mask_ref, kn_ref, vn_ref, bn_ref, maskn_ref, o_ref,
     m_ref, l_ref, acc_ref) = refs[2 * g_pages:]
    g = pl.program_id(1)
    keys_g = mask_ref.shape[1]
    dh = q_ref.shape[3]

    @pl.when(g == 0)
    def _():
        m_ref[...] = jnp.full(m_ref.shape, NEG_INF, F32)
        l_ref[...] = jnp.zeros(l_ref.shape, F32)
        acc_ref[...] = jnp.zeros(acc_ref.shape, F32)

    def update(gi, s, v_rows):
        m_old = m_ref[gi]
        m_new = jnp.maximum(m_old, jnp.max(s, axis=-1, keepdims=True))
        alpha = jnp.exp(m_old - m_new)
        p = jnp.exp(s - m_new)
        l_ref[gi] = alpha * l_ref[gi] + jnp.sum(p, axis=-1, keepdims=True)
        acc_ref[gi] = alpha * acc_ref[gi] + _dot(p.astype(BF16), v_rows)
        m_ref[gi] = m_new

    def group_rows(page_ref, gi):
        tiles = page_ref[0, pl.ds(gi, PAGE_SIZE, stride=n_grp), :, :]
        return tiles.reshape(keys_g, dh).astype(BF16)

    steps_per_block = SUBLANES // g_pages
    bias_rows = [bias_ref[0, pl.ds((g % steps_per_block) * g_pages + e, 1), :]
                 for e in range(g_pages)]

    def group_scores(gi):
        parts = [_dot_nt(q_ref[0, gi], group_rows(k_refs[e], gi))
                 + (mask_ref[...] + bias_rows[e][:, gi * keys_g:(gi + 1) * keys_g])
                 for e in range(g_pages)]
        return jnp.concatenate(parts, axis=1)

    scores = [group_scores(gi) for gi in range(n_grp)]
    for gi in range(n_grp):
        values = jnp.concatenate([group_rows(v_refs[e], gi) for e in range(g_pages)], axis=0)
        update(gi, scores[gi], values)

    @pl.when(g == n_steps - 1)
    def _():
        for gi in range(n_grp):
            s = _dot_nt(q_ref[0, gi], kn_ref[0, gi]) + (maskn_ref[...] + bn_ref[0, gi])
            update(gi, s, vn_ref[0, gi])
        o_ref[0] = (acc_ref[...] / l_ref[...]).astype(o_ref.dtype)


def _sattn_call(page_table, q4, cache_k, cache_v, bias, mask, k_new, v_new,
                bias_new, mask_new):
    b, n_grp, rows, dh = q4.shape
    n_pages = page_table.shape[1]
    n_steps = n_pages // PAGES_PER_STEP
    page_tiles = cache_k.shape[1]
    width = bias.shape[2]
    steps_per_block = SUBLANES // PAGES_PER_STEP

    def page_map(e):
        return lambda bi, g, pt: (pt[bi, g * PAGES_PER_STEP + e], 0, 0, 0)

    per_seq = lambda bi, g, pt: (bi, 0, 0, 0)
    const2 = lambda bi, g, pt: (0, 0)
    page_spec = lambda e: pl.BlockSpec((1, page_tiles, SUBLANES, dh), page_map(e))
    in_specs = [pl.BlockSpec((1, n_grp, rows, dh), per_seq)]
    in_specs += [page_spec(e) for e in range(PAGES_PER_STEP)]
    in_specs += [page_spec(e) for e in range(PAGES_PER_STEP)]
    in_specs += [
        pl.BlockSpec((1, SUBLANES, width), lambda bi, g, pt: (bi, g // steps_per_block, 0)),
        pl.BlockSpec(mask.shape, const2),
        pl.BlockSpec((1, n_grp, rows, dh), per_seq),
        pl.BlockSpec((1, n_grp, rows, dh), per_seq),
        pl.BlockSpec((1, n_grp, 1, rows), per_seq),
        pl.BlockSpec(mask_new.shape, const2),
    ]
    grid_spec = pltpu.PrefetchScalarGridSpec(
        num_scalar_prefetch=1,
        grid=(b, n_steps),
        in_specs=in_specs,
        out_specs=pl.BlockSpec((1, n_grp, rows, dh), per_seq),
        scratch_shapes=[pltpu.VMEM((n_grp, rows, 1), F32), pltpu.VMEM((n_grp, rows, 1), F32),
                        pltpu.VMEM((n_grp, rows, dh), F32)],
    )
    kern = functools.partial(_sattn_kernel, n_steps=n_steps, n_grp=n_grp)
    return pl.pallas_call(
        kern,
        grid_spec=grid_spec,
        out_shape=jax.ShapeDtypeStruct((b, n_grp, rows, dh), BF16),
        compiler_params=_params(2),
        name="fox_decode",
    )(page_table, q4, *([cache_k] * PAGES_PER_STEP), *([cache_v] * PAGES_PER_STEP),
      bias, mask, k_new, v_new, bias_new, mask_new)


def _oproj_kernel(a_ref, w_ref, h_ref, o_ref):
    o_ref[...] = h_ref[...] + _dot(a_ref[...], w_ref[...])


def _oproj_call(a, w_o, h, tm, tn):
    m, d = h.shape
    return pl.pallas_call(
        _oproj_kernel,
        grid=(m // tm, d // tn),
        in_specs=[
            pl.BlockSpec((tm, d), lambda i, j: (i, 0)),
            pl.BlockSpec((d, tn), lambda i, j: (0, j)),
            pl.BlockSpec((tm, tn), lambda i, j: (i, j)),
        ],
        out_specs=pl.BlockSpec((tm, tn), lambda i, j: (i, j)),
        out_shape=jax.ShapeDtypeStruct((m, d), F32),
        compiler_params=_params(2),
        name="o_proj",
    )(a, w_o, h)


def _cumsum_matrix(n, group):
    t = np.arange(n)[:, None]
    tp = np.arange(n)[None, :]
    return ((tp <= t) & (tp // group == t // group)).astype(BF16)


def _placement(nh, d):
    lane = np.arange(LANES)[None, :, None]
    col = np.arange(d)[None, None, :]
    piece = np.arange(N_SPLIT)[:, None, None]
    return ((lane < nh) & (col == lane * HEAD_DIM + piece)).astype(BF16)


def kernel(x_prompt, x_sample, state_conv_a, state_ffn, cache_k, cache_v, cache_logf, page_table, p_prompt, p_sample, g_mix_a, w_pw1, w_dwa, b_dwa, ln_g, ln_b, w_pw2, g_kv, w_k, w_v, g_k, w_f, b_f, g_mix_b, w_q, g_q, w_o, g_ffn, w_up, w_gate, w_fdw, b_fdw, w_down, g_ple, w_ple_gate, w_ple_proj):
    n_p, t_p, d = x_prompt.shape
    n_s, t_s, _ = x_sample.shape
    depth = g_ffn.shape[0]
    n_a = g_mix_a.shape[0]
    f = w_up.shape[2]
    nh = d // HEAD_DIM
    m_s = n_s * t_s

    bf = lambda w: w.astype(BF16)
    layers = lambda w: [bf(w[i]) for i in range(w.shape[0])]
    w_pw1_b, w_pw2_b = layers(w_pw1), layers(w_pw2)
    w_k_b, w_v_b, w_q_b, w_o_b = bf(w_k), bf(w_v), layers(w_q), layers(w_o)
    w_up_b, w_gate_b, w_down_b = bf(w_up), bf(w_gate), bf(w_down)
    w_pg_b, w_pp_b = bf(w_ple_gate), bf(w_ple_proj)
    w_f_pad = jnp.pad(bf(w_f), ((0, 0), (0, LANES - nh)))
    b_f_pad = jnp.pad(b_f, (0, LANES - nh)).reshape(1, LANES)
    place = _placement(nh, d)
    row = lambda a: a.reshape(1, -1)

    def trunk(x, p, n_seq, seq_len, a_prev32, tiles, ffn, attend, tri, transposed):
        h = x.reshape(n_seq * seq_len, d)
        a_us, f_bufs = [], []
        k = v = lf = None
        for i in range(depth):
            attn = None
            if i < n_a:
                u = _glu_call(h, row(g_mix_a[i]), w_pw1_b[i], tiles.rows, tiles.cols)
                h = _conva_call(u, a_prev32[i], w_dwa[i], row(b_dwa[i]), row(ln_g[i]),
                                row(ln_b[i]), w_pw2_b[i], h, tiles.rows_conv, seq_len)
                a_us.append(u.reshape(n_seq, seq_len, d))
            else:
                jb = i - n_a
                outs = _qkv_call(
                    h, row(g_kv), row(g_mix_b[jb]), w_k_b, w_v_b, w_q_b[jb],
                    row(g_k), row(g_q[jb]), w_f_pad, b_f_pad, tri, place,
                    tiles.rows_qkv, tiles.cols, seq_len, transposed)
                k, v = outs[0], outs[1]
                lf = outs[6] if transposed else outs[5]
                attn = (attend(outs), w_o_b[jb])
            h, fb = ffn(i, h, attn)
            f_bufs.append(fb)
            h = _ple_call(h, row(g_ple[i]), p[i].reshape(n_seq * seq_len, -1),
                          w_pg_b, w_pp_b, tiles.rows_ple, i)
        shp = (n_seq, seq_len, nh, HEAD_DIM)
        return (h.reshape(n_seq, seq_len, d), a_us, jnp.stack(f_bufs),
                k.reshape(shp), v.reshape(shp),
                lf[:, :nh].reshape(n_seq, seq_len, nh))

    tp = PROMPT_TILES

    def ffn_prompt(i, h, attn):
        prev = jnp.zeros((n_p, CONV_F_WIDTH - 1, f), F32)
        return _ffn_seq_call(h, row(g_ffn[i]), w_up_b, w_gate_b, w_fdw[i],
                             row(b_fdw[i]), w_down_b, prev, tp.rows_ffn, tp.cols, t_p, i,
                             attn)

    def attend_prompt(outs):
        _, _, kb, kx, v_t, q_t, _ = outs
        return _attn_call(q_t, kb, kx, v_t, n_p, t_p, tp.rows_qkv)

    a_prev_p = [jnp.zeros((n_p, HALO_ROWS, d), F32) for _ in range(n_a)]
    (y_p, a_us_p, ffn_p, k_p, v_p, lf_p) = trunk(
        x_prompt, p_prompt, n_p, t_p, a_prev_p, tp, ffn_prompt,
        attend_prompt, _cumsum_matrix(tp.rows_qkv, tp.rows_qkv), True)
    conv_a_p = jnp.stack([u[:, t_p - (CONV_A_WIDTH - 1):] for u in a_us_p])

    def ffn_sample(i, h, attn):
        if attn is not None:
            h = _oproj_call(attn[0], attn[1], h, m_s, tp.cols)
        st = state_ffn[i]
        z = jnp.zeros((n_s, t_s, f), F32)
        p1 = z.at[:, 0].set(st[:, 1]).reshape(m_s, f)
        p2 = z.at[:, 0].set(st[:, 0]).at[:, 1].set(st[:, 1]).reshape(m_s, f)
        h, u = _ffn_multi_call(h, row(g_ffn[i]), w_up_b, w_gate_b, w_fdw[i],
                               row(b_fdw[i]), w_down_b, p1, p2, tp.cols, t_s, i)
        full = jnp.concatenate([st, u.reshape(n_s, t_s, f)], axis=1)
        return h, full[:, -(CONV_F_WIDTH - 1):]

    n_pool = cache_k.shape[0]
    n_pages = page_table.shape[1]
    n_grp = nh // SUBLANES
    keys_g = PAGE_SIZE * SUBLANES
    page_rows = PAGE_SIZE * nh
    tiles_shape = (n_pool, PAGE_SIZE * n_grp, SUBLANES, HEAD_DIM)
    cache_k4 = cache_k.reshape(tiles_shape)
    cache_v4 = cache_v.reshape(tiles_shape)
    lf_flat = cache_logf.reshape(n_pool, 1, page_rows)
    fr = np.arange(page_rows)
    t_in, h_in = fr // nh, fr % nh
    t_out = (fr % keys_g) // SUBLANES
    h_out = (fr // keys_g) * SUBLANES + fr % SUBLANES
    lane = np.arange(LANES)
    sfx = ((h_in[:, None] == h_out[None, :]) & (t_in[:, None] > t_out[None, :])).astype(BF16)
    tot_m = (h_in[:, None] == lane[None, :]).astype(BF16)
    pg = np.arange(n_pages)
    later_m = (pg[None, :] > pg[:, None]).astype(BF16)
    spread = (lane[:, None] == h_out[None, :]).astype(BF16)
    qrow = np.arange(t_s * SUBLANES)
    kcol = np.arange(keys_g)
    mask = np.where((qrow[:, None] % SUBLANES) == (kcol[None, :] % SUBLANES),
                    0.0, NEG_INF).astype(np.float32)
    mask_new = np.where(((qrow[:, None] % SUBLANES) == (qrow[None, :] % SUBLANES))
                        & (qrow[None, :] // SUBLANES <= qrow[:, None] // SUBLANES),
                        0.0, NEG_INF).astype(np.float32)
    bias = _bias_call(page_table, lf_flat, sfx, tot_m, later_m, spread)

    def by_group(a, width):
        a5 = a.reshape(n_s, t_s, n_grp, SUBLANES, width)
        return jnp.transpose(a5, (0, 2, 1, 3, 4)).reshape(n_s, n_grp, t_s * SUBLANES, width)

    def attend_sample(outs):
        _, _, kb, vb, q, _, c = outs
        bias_new = -by_group(c[:, :nh], 1).reshape(n_s, n_grp, 1, t_s * SUBLANES)
        a = _sattn_call(page_table, by_group(q, HEAD_DIM), cache_k4, cache_v4, bias, mask,
                        by_group(kb, HEAD_DIM), by_group(vb, HEAD_DIM), bias_new, mask_new)
        a5 = a.reshape(n_s, n_grp, t_s, SUBLANES, HEAD_DIM)
        return jnp.transpose(a5, (0, 2, 1, 3, 4)).reshape(m_s, d)

    a_prev_s = [jnp.pad(state_conv_a[i], ((0, 0), (HALO_ROWS - (CONV_A_WIDTH - 1), 0), (0, 0)))
                for i in range(n_a)]
    ts = _Tiles(rows=m_s, rows_qkv=m_s, rows_ffn=m_s, rows_conv=t_s, rows_ple=m_s,
                cols=tp.cols)
    (y_s, a_us_s, ffn_s, k_s, v_s, lf_s) = trunk(
        x_sample, p_sample, n_s, t_s, a_prev_s, ts, ffn_sample,
        attend_sample, _cumsum_matrix(m_s, t_s), False)
    conv_a_s = jnp.stack([
        jnp.concatenate([state_conv_a[i], a_us_s[i]], axis=1)[:, -(CONV_A_WIDTH - 1):]
        for i in range(n_a)])

    return (y_p, y_s, conv_a_p, ffn_p, k_p, v_p, lf_p,
            conv_a_s, ffn_s, k_s, v_s, lf_s)
```

```python
import functools
from typing import NamedTuple

import jax
import jax.numpy as jnp
import numpy as np
from jax import lax
from jax.experimental import pallas as pl
from jax.experimental.pallas import tpu as pltpu

F32 = jnp.float32
BF16 = jnp.bfloat16

HEAD_DIM = 128
CONV_A_WIDTH = 31
CONV_F_WIDTH = 3
PAGE_SIZE = 128
EPS = 1e-6
NEG_INF = -1e30
LOG2_E = 1.4426950408889634

LANES = 128
SUBLANES = 8
HALO_ROWS = 32
VMEM_LIMIT = 56 * 1024 * 1024
PAGES_PER_STEP = 8
BIAS_PAGES_PER_STEP = 32
N_SPLIT = 3
MXU_COLS = 256


class _Tiles(NamedTuple):
    rows: int
    rows_qkv: int
    rows_ffn: int
    rows_conv: int
    rows_ple: int
    cols: int


PROMPT_TILES = _Tiles(rows=1024, rows_qkv=512, rows_ffn=512, rows_conv=256,
                      rows_ple=512, cols=512)


def _params(n_axes):
    return pltpu.CompilerParams(
        dimension_semantics=("arbitrary",) * n_axes,
        vmem_limit_bytes=VMEM_LIMIT)


def _dot(a, b):
    return jnp.dot(a, b, preferred_element_type=F32)


def _dot_nt(a, b):
    return lax.dot_general(a, b, (((1,), (1,)), ((), ())),
                           preferred_element_type=F32)


def _sigmoid(x):
    return jax.nn.sigmoid(x)


def _log_sigmoid(z):
    return -(jnp.maximum(-z, 0.0) + jnp.log1p(jnp.exp(-jnp.abs(z))))


def _rms_rows(x):
    ms = jnp.mean(x * x, axis=-1, keepdims=True)
    return x * lax.rsqrt(ms + EPS)


def _split3(x):
    hi = x.astype(BF16)
    r1 = x - hi.astype(F32)
    mid = r1.astype(BF16)
    lo = (r1 - mid.astype(F32)).astype(BF16)
    return hi, mid, lo


def _dot3(x, w_bf16):
    hi, mid, lo = _split3(x)
    return _dot(hi, w_bf16) + _dot(mid, w_bf16) + _dot(lo, w_bf16)


def _dot3_left(w_bf16, x):
    hi, mid, lo = _split3(x)
    return _dot(w_bf16, hi) + _dot(w_bf16, mid) + _dot(w_bf16, lo)


def _glu_kernel(x_ref, g_ref, w1_ref, w2_ref, u_ref, xn_ref):
    @pl.when(pl.program_id(1) == 0)
    def _():
        xn_ref[...] = (_rms_rows(x_ref[...]) * g_ref[...]).astype(BF16)

    xn = xn_ref[...]
    for c0 in range(0, u_ref.shape[1], MXU_COLS):
        cols = slice(c0, c0 + MXU_COLS)
        a1 = _dot(xn, w1_ref[:, cols])
        a2 = _dot(xn, w2_ref[:, cols])
        u_ref[:, cols] = a1 * _sigmoid(a2)


def _glu_call(h, g, w_pw1, tm, tn):
    m, d = h.shape
    nj = d // tn
    return pl.pallas_call(
        _glu_kernel,
        grid=(m // tm, nj),
        in_specs=[
            pl.BlockSpec((tm, d), lambda i, j: (i, 0)),
            pl.BlockSpec((1, d), lambda i, j: (0, 0)),
            pl.BlockSpec((d, tn), lambda i, j: (0, j)),
            pl.BlockSpec((d, tn), lambda i, j: (0, j + nj)),
        ],
        out_specs=pl.BlockSpec((tm, tn), lambda i, j: (i, j)),
        out_shape=jax.ShapeDtypeStruct((m, d), F32),
        scratch_shapes=[pltpu.VMEM((tm, d), BF16)],
        compiler_params=_params(2),
        name="glu",
    )(h, g, w_pw1, w_pw1)


def _conva_kernel(u_ref, halo_ref, prev_ref, wdw_ref, bdw_ref, lng_ref, lnb_ref,
                  w2_ref, h_ref, o_ref, full_ref, c_ref, *, tm, tps, rb):
    i = pl.program_id(0)
    d = u_ref.shape[1]
    first = (i % tps) == 0
    full_ref[0:HALO_ROWS, :] = jnp.where(first, prev_ref[0], halo_ref[...])
    full_ref[HALO_ROWS:HALO_ROWS + tm, :] = u_ref[...]
    full_ref[HALO_ROWS + tm:HALO_ROWS + tm + SUBLANES, :] = jnp.zeros((SUBLANES, d), F32)
    off = HALO_ROWS - (CONV_A_WIDTH - 1)

    def conv_chunk(r0, c0):
        cols = slice(c0, c0 + LANES)
        acc = jnp.broadcast_to(bdw_ref[:, cols], (rb, LANES))
        for b in range(SUBLANES):
            z = None
            for j in range(CONV_A_WIDTH):
                if (off + j) % SUBLANES != b:
                    continue
                s = r0 + (off + j) // SUBLANES * SUBLANES
                term = wdw_ref[j:j + 1, cols] * full_ref[s:s + rb + SUBLANES, cols]
                z = term if z is None else z + term
            acc = acc + z[b:b + rb, :]
        c_ref[r0:r0 + rb, cols] = acc

    for r0 in range(0, tm, rb):
        for c0 in range(0, d, LANES):
            conv_chunk(r0, c0)
    c = c_ref[...]
    mu = jnp.mean(c, axis=-1, keepdims=True)
    xc = c - mu
    var = jnp.mean(xc * xc, axis=-1, keepdims=True)
    y = xc * lax.rsqrt(var + EPS) * lng_ref[...] + lnb_ref[...]
    a = (y * _sigmoid(y)).astype(BF16)
    o_ref[...] = h_ref[...] + _dot(a, w2_ref[...])


def _conva_call(u, prev32, w_dw, b_dw, ln_g, ln_b, w_pw2, h, tm, seq_len):
    m, d = u.shape
    tps = max(seq_len // tm, 1)
    rb = min(tm, 128)
    if tm % HALO_ROWS == 0:
        k = tm // HALO_ROWS
        halo_map = lambda i: (jnp.maximum(i * k - 1, 0), 0)
    else:
        assert tps == 1
        halo_map = lambda i: (0, 0)
    kern = functools.partial(_conva_kernel, tm=tm, tps=tps, rb=rb)
    return pl.pallas_call(
        kern,
        grid=(m // tm,),
        in_specs=[
            pl.BlockSpec((tm, d), lambda i: (i, 0)),
            pl.BlockSpec((HALO_ROWS, d), halo_map),
            pl.BlockSpec((1, HALO_ROWS, d), lambda i: (i // tps, 0, 0)),
            pl.BlockSpec((CONV_A_WIDTH, d), lambda i: (0, 0)),
            pl.BlockSpec((1, d), lambda i: (0, 0)),
            pl.BlockSpec((1, d), lambda i: (0, 0)),
            pl.BlockSpec((1, d), lambda i: (0, 0)),
            pl.BlockSpec((d, d), lambda i: (0, 0), pipeline_mode=pl.Buffered(1)),
            pl.BlockSpec((tm, d), lambda i: (i, 0)),
        ],
        out_specs=pl.BlockSpec((tm, d), lambda i: (i, 0)),
        out_shape=jax.ShapeDtypeStruct((m, d), F32),
        scratch_shapes=[pltpu.VMEM((HALO_ROWS + tm + SUBLANES, d), F32),
                        pltpu.VMEM((tm, d), F32)],
        compiler_params=_params(1),
        name="conv_a",
    )(u, u, prev32, w_dw, b_dw, ln_g, ln_b, w_pw2, h)


def _ffn_body(h_ref, g_ref, wu_ref, wg_ref, wdw_ref, bdw_ref, wd_ref,
              o_ref, xn_ref, ubuf_ref, tm, prev_rows, fix_shifted):
    xn = xn_ref[...]
    u = _dot(xn, wu_ref[...])
    ubuf_ref[SUBLANES:SUBLANES + tm, :] = u
    ubuf_ref[SUBLANES - 2:SUBLANES, :] = prev_rows
    u1 = ubuf_ref[SUBLANES - 1:SUBLANES - 1 + tm, :]
    u2 = ubuf_ref[SUBLANES - 2:SUBLANES - 2 + tm, :]
    u1, u2 = fix_shifted(u1, u2)
    c = (wdw_ref[0:1, :] * u2 + wdw_ref[1:2, :] * u1 + wdw_ref[2:3, :] * u
         + bdw_ref[...])
    gt = _dot(xn, wg_ref[...])
    act = (c * _sigmoid(c) * gt).astype(BF16)
    o_ref[...] += _dot(act, wd_ref[...])


def _ffn_seq_kernel(h_ref, g_ref, wu_ref, wg_ref, wdw_ref, bdw_ref, wd_ref,
                    *rest, tm, tps, with_attn):
    if with_attn:
        a_ref, wo_ref, *rest = rest
    prev_ref, o_ref, fb_ref, xn_ref, ubuf_ref, carry_ref = rest
    i = pl.program_id(0)
    j = pl.program_id(1)
    first = (i % tps) == 0

    @pl.when(j == 0)
    def _():
        h = h_ref[...]
        if with_attn:
            h = h + _dot(a_ref[...], wo_ref[...])
        xn_ref[...] = (_rms_rows(h) * g_ref[...]).astype(BF16)
        o_ref[...] = h

        @pl.when(i == 0)
        def _():
            carry_ref[...] = jnp.zeros(carry_ref.shape, F32)

    prev_rows = jnp.where(first, prev_ref[0], carry_ref[j])
    _ffn_body(h_ref, g_ref, wu_ref, wg_ref, wdw_ref, bdw_ref, wd_ref,
              o_ref, xn_ref, ubuf_ref, tm, prev_rows, lambda a, b: (a, b))
    tail = ubuf_ref[SUBLANES + tm - 2:SUBLANES + tm, :]
    carry_ref[j] = tail

    @pl.when((i % tps) == tps - 1)
    def _():
        fb_ref[i // tps, j] = tail


def _ffn_multi_kernel(h_ref, g_ref, wu_ref, wg_ref, wdw_ref, bdw_ref, wd_ref,
                      p1_ref, p2_ref, o_ref, u_ref, xn_ref, ubuf_ref,
                      *, tm, seq_len):
    @pl.when(pl.program_id(1) == 0)
    def _():
        xn_ref[...] = (_rms_rows(h_ref[...]) * g_ref[...]).astype(BF16)
        o_ref[...] = h_ref[...]
        ubuf_ref[0:SUBLANES, :] = jnp.zeros((SUBLANES, ubuf_ref.shape[1]), F32)

    def fix_shifted(u1, u2):
        t = lax.broadcasted_iota(jnp.int32, (tm, 1), 0) % seq_len
        return (jnp.where(t >= 1, u1, p1_ref[...]),
                jnp.where(t >= 2, u2, p2_ref[...]))

    prev_rows = jnp.zeros((CONV_F_WIDTH - 1, ubuf_ref.shape[1]), F32)
    _ffn_body(h_ref, g_ref, wu_ref, wg_ref, wdw_ref, bdw_ref, wd_ref,
              o_ref, xn_ref, ubuf_ref, tm, prev_rows, fix_shifted)
    u_ref[...] = ubuf_ref[SUBLANES:SUBLANES + tm, :]


def _ffn_common_specs(tm, tn, d, f, layer):
    return [
        pl.BlockSpec((tm, d), lambda i, j: (i, 0)),
        pl.BlockSpec((1, d), lambda i, j: (0, 0)),
        pl.BlockSpec((None, d, tn), lambda i, j: (layer, 0, j)),
        pl.BlockSpec((None, d, tn), lambda i, j: (layer, 0, j)),
        pl.BlockSpec((CONV_F_WIDTH, tn), lambda i, j: (0, j)),
        pl.BlockSpec((1, tn), lambda i, j: (0, j)),
        pl.BlockSpec((None, tn, d), lambda i, j: (layer, j, 0)),
    ]


def _ffn_seq_call(h, g, w_up, w_gate, w_dw, b_dw, w_down, prev, tm, tn, seq_len, layer,
                  attn=None):
    m, d = h.shape
    f = w_up.shape[2]
    n_seq = m // seq_len
    tps = seq_len // tm
    nj = f // tn
    kern = functools.partial(_ffn_seq_kernel, tm=tm, tps=tps, with_attn=attn is not None)
    attn_specs = [] if attn is None else [
        pl.BlockSpec((tm, d), lambda i, j: (i, 0)),
        pl.BlockSpec((d, d), lambda i, j: (0, 0), pipeline_mode=pl.Buffered(1)),
    ]
    h_new, fb = pl.pallas_call(
        kern,
        grid=(m // tm, nj),
        in_specs=_ffn_common_specs(tm, tn, d, f, layer) + attn_specs + [
            pl.BlockSpec((1, CONV_F_WIDTH - 1, tn), lambda i, j: (i // tps, 0, j)),
        ],
        out_specs=[
            pl.BlockSpec((tm, d), lambda i, j: (i, 0)),
            pl.BlockSpec((n_seq, nj, CONV_F_WIDTH - 1, tn), lambda i, j: (0, 0, 0, 0)),
        ],
        out_shape=[jax.ShapeDtypeStruct((m, d), F32),
                   jax.ShapeDtypeStruct((n_seq, nj, CONV_F_WIDTH - 1, tn), F32)],
        scratch_shapes=[pltpu.VMEM((tm, d), BF16),
                        pltpu.VMEM((SUBLANES + tm, tn), F32),
                        pltpu.VMEM((nj, CONV_F_WIDTH - 1, tn), F32)],
        compiler_params=_params(2),
        name="ffn_seq",
    )(h, g, w_up, w_gate, w_dw, b_dw, w_down, *(attn or ()), prev)
    fb = jnp.transpose(fb, (0, 2, 1, 3)).reshape(n_seq, CONV_F_WIDTH - 1, f)
    return h_new, fb


def _ffn_multi_call(h, g, w_up, w_gate, w_dw, b_dw, w_down, p1, p2, tn, seq_len, layer):
    m, d = h.shape
    f = w_up.shape[2]
    kern = functools.partial(_ffn_multi_kernel, tm=m, seq_len=seq_len)
    return pl.pallas_call(
        kern,
        grid=(1, f // tn),
        in_specs=_ffn_common_specs(m, tn, d, f, layer) + [
            pl.BlockSpec((m, tn), lambda i, j: (0, j)),
            pl.BlockSpec((m, tn), lambda i, j: (0, j)),
        ],
        out_specs=[
            pl.BlockSpec((m, d), lambda i, j: (0, 0)),
            pl.BlockSpec((m, tn), lambda i, j: (0, j)),
        ],
        out_shape=[jax.ShapeDtypeStruct((m, d), F32),
                   jax.ShapeDtypeStruct((m, f), F32)],
        scratch_shapes=[pltpu.VMEM((m, d), BF16),
                        pltpu.VMEM((SUBLANES + m, tn), F32)],
        compiler_params=_params(2),
        name="ffn_multi",
    )(h, g, w_up, w_gate, w_dw, b_dw, w_down, p1, p2)


def _ple_kernel(h_ref, g_ref, p_ref, wg_ref, wp_ref, o_ref):
    h = h_ref[...]
    xn = (_rms_rows(h) * g_ref[...]).astype(BF16)
    gate = _dot(xn, wg_ref[...])
    proj = _dot(p_ref[...].astype(BF16), wp_ref[...])
    o_ref[...] = h + _sigmoid(gate) * proj


def _ple_call(h, g, p, w_gate, w_proj, tm, layer):
    m, d = h.shape
    pd = p.shape[1]
    resident = pl.Buffered(1)
    return pl.pallas_call(
        _ple_kernel,
        grid=(m // tm,),
        in_specs=[
            pl.BlockSpec((tm, d), lambda i: (i, 0)),
            pl.BlockSpec((1, d), lambda i: (0, 0)),
            pl.BlockSpec((tm, pd), lambda i: (i, 0)),
            pl.BlockSpec((None, d, d), lambda i: (layer, 0, 0), pipeline_mode=resident),
            pl.BlockSpec((None, pd, d), lambda i: (layer, 0, 0), pipeline_mode=resident),
        ],
        out_specs=pl.BlockSpec((tm, d), lambda i: (i, 0)),
        out_shape=jax.ShapeDtypeStruct((m, d), F32),
        compiler_params=_params(1),
        name="ple",
    )(h, g, p, w_gate, w_proj)


def _qkv_kernel(h_ref, gkv_ref, gmix_ref, wk_ref, wv_ref, wq_ref, gk_ref, gq_ref,
                wf_ref, bf_ref, tri_ref, place_ref, *refs, tm, tn, tps, transposed):
    if transposed:
        (k_ref, v_ref, kb_ref, kx_ref, vt_ref, qt_ref, lf_ref,
         xkv_ref, xq_ref, carry_ref, cp_ref) = refs
    else:
        (k_ref, v_ref, kb_ref, vb_ref, q_ref, lf_ref, c_ref,
         xkv_ref, xq_ref, carry_ref) = refs
    i = pl.program_id(0)
    j = pl.program_id(1)

    @pl.when(j == 0)
    def _():
        r = _rms_rows(h_ref[...])
        xkv = (r * gkv_ref[...]).astype(BF16)
        xkv_ref[...] = xkv
        xq_ref[...] = (r * gmix_ref[...]).astype(BF16)
        lf = _log_sigmoid(_dot(xkv, wf_ref[...]) + bf_ref[...])
        lf_ref[...] = lf

        @pl.when((i % tps) == 0)
        def _():
            carry_ref[...] = jnp.zeros(carry_ref.shape, F32)

        c = _dot3_left(tri_ref[...], lf) + carry_ref[...]
        carry_ref[...] = c[tm - 1:tm, :]
        if transposed:
            for piece, val in enumerate(_split3(-LOG2_E * c)):
                cp_ref[piece] = val
        else:
            c_ref[...] = c

    xkv = xkv_ref[...]
    kk = _dot(xkv, wk_ref[...])
    vv = _dot(xkv, wv_ref[...])
    qq = _dot(xq_ref[...], wq_ref[...])
    v_ref[...] = vv
    scale = HEAD_DIM ** -0.5 * (LOG2_E if transposed else 1.0)
    qn_heads = []
    for c0 in range(0, tn, HEAD_DIM):
        sl = slice(c0, c0 + HEAD_DIM)
        kn = _rms_rows(kk[:, sl]) * gk_ref[...]
        k_ref[:, sl] = kn
        kb_ref[:, sl] = kn.astype(BF16)
        qn_heads.append(_rms_rows(qq[:, sl]) * gq_ref[...] * scale)
    qn = jnp.concatenate(qn_heads, axis=1)
    if transposed:
        vt_ref[0] = vv.T.astype(BF16)
        qt_ref[0] = qn.T.astype(BF16)
        kx = _dot(cp_ref[0], place_ref[0])
        for piece in range(1, N_SPLIT):
            kx = kx + _dot(cp_ref[piece], place_ref[piece])
        kx_ref[...] = kx.astype(BF16)
    else:
        vb_ref[...] = vv.astype(BF16)
        q_ref[...] = qn.astype(BF16)


def _qkv_call(h, g_kv, g_mix, w_k, w_v, w_q, g_k, g_q, w_f_pad, b_f_pad,
              tri, place, tm, tn, seq_len, transposed):
    m, d = h.shape
    tps = max(seq_len // tm, 1)
    kern = functools.partial(_qkv_kernel, tm=tm, tn=tn, tps=tps, transposed=transposed)
    row = lambda i, j: (i, 0)
    const = lambda i, j: (0, 0)
    col = lambda i, j: (0, j)
    tile = lambda i, j: (i, j)
    tile_t = lambda i, j: (i, j, 0)
    in_specs = [
        pl.BlockSpec((tm, d), row),
        pl.BlockSpec((1, d), const),
        pl.BlockSpec((1, d), const),
        pl.BlockSpec((d, tn), col),
        pl.BlockSpec((d, tn), col),
        pl.BlockSpec((d, tn), col),
        pl.BlockSpec((1, HEAD_DIM), const),
        pl.BlockSpec((1, HEAD_DIM), const),
        pl.BlockSpec((d, LANES), const),
        pl.BlockSpec((1, LANES), const),
        pl.BlockSpec((tm, tm), const),
        pl.BlockSpec((N_SPLIT, LANES, tn), lambda i, j: (0, 0, j)),
    ]
    scratch = [pltpu.VMEM((tm, d), BF16), pltpu.VMEM((tm, d), BF16),
               pltpu.VMEM((1, LANES), F32)]
    f32_md = jax.ShapeDtypeStruct((m, d), F32)
    bf_md = jax.ShapeDtypeStruct((m, d), BF16)
    if transposed:
        out_specs = [
            pl.BlockSpec((tm, tn), tile), pl.BlockSpec((tm, tn), tile),
            pl.BlockSpec((tm, tn), tile), pl.BlockSpec((tm, tn), tile),
            pl.BlockSpec((1, tn, tm), tile_t), pl.BlockSpec((1, tn, tm), tile_t),
            pl.BlockSpec((tm, LANES), row),
        ]
        bf_t = jax.ShapeDtypeStruct((m // tm, d, tm), BF16)
        out_shape = [f32_md, f32_md, bf_md, bf_md, bf_t, bf_t,
                     jax.ShapeDtypeStruct((m, LANES), F32)]
        scratch.append(pltpu.VMEM((N_SPLIT, tm, LANES), BF16))
    else:
        out_specs = [
            pl.BlockSpec((tm, tn), tile), pl.BlockSpec((tm, tn), tile),
            pl.BlockSpec((tm, tn), tile), pl.BlockSpec((tm, tn), tile),
            pl.BlockSpec((tm, tn), tile),
            pl.BlockSpec((tm, LANES), row), pl.BlockSpec((tm, LANES), row),
        ]
        out_shape = [f32_md, f32_md, bf_md, bf_md, bf_md,
                     jax.ShapeDtypeStruct((m, LANES), F32),
                     jax.ShapeDtypeStruct((m, LANES), F32)]
    return pl.pallas_call(
        kern,
        grid=(m // tm, d // tn),
        in_specs=in_specs,
        out_specs=out_specs,
        out_shape=out_shape,
        scratch_shapes=scratch,
        compiler_params=_params(2),
        name="qkv_t" if transposed else "qkv",
    )(h, g_kv, g_mix, w_k, w_v, w_q, g_k, g_q, w_f_pad, b_f_pad, tri, place)


def _attn_kernel(q_ref, k_ref, kx_ref, v_ref, o_ref,
                 s0_ref, s1_ref, bm0_ref, bm1_ref, m_ref, acc_ref, *, tq, nb):
    ones_rows = (lax.broadcasted_iota(jnp.int32, (HEAD_DIM, tq), 0) < N_SPLIT).astype(BF16)
    sum_rows = jnp.ones((acc_ref.shape[0] - HEAD_DIM, tq), BF16)
    s_refs = (s0_ref, s1_ref)
    bm_refs = (bm0_ref, bm1_ref)

    def n_keys(qi, kb, c0):
        return min(tq, c0 + MXU_COLS) if kb == qi else tq

    def scores(qi, kb, slot):
        q_aug = jnp.concatenate([q_ref[qi], ones_rows], axis=0)
        rows = slice(kb * tq, (kb + 1) * tq)
        k_aug = jnp.concatenate([k_ref[rows, :], kx_ref[rows, :]], axis=1)
        for c0 in range(0, tq, MXU_COLS):
            cols = slice(c0, c0 + MXU_COLS)
            nk = n_keys(qi, kb, c0)
            s = _dot(k_aug[0:nk, :], q_aug[:, cols])
            if kb == qi:
                key = lax.broadcasted_iota(jnp.int32, s.shape, 0)
                qry = lax.broadcasted_iota(jnp.int32, s.shape, 1) + c0
                s = jnp.where(key <= qry, s, NEG_INF)
            s_refs[slot][0:nk, cols] = s
            bm_refs[slot][:, cols] = jnp.max(s, axis=0, keepdims=True)

    def accumulate(qi, kb, slot):
        v_aug = jnp.concatenate([v_ref[kb], sum_rows], axis=0)
        for c0 in range(0, tq, MXU_COLS):
            cols = slice(c0, c0 + MXU_COLS)
            nk = n_keys(qi, kb, c0)
            if kb == 0:
                m_new = bm_refs[slot][:, cols]
                p = jnp.exp2((s_refs[slot][0:nk, cols] - m_new).astype(BF16))
                acc_ref[:, cols] = _dot(v_aug[:, 0:nk], p)
            else:
                m_old = m_ref[:, cols]
                m_new = jnp.maximum(m_old, bm_refs[slot][:, cols])
                alpha = jnp.exp2(m_old - m_new)
                p = jnp.exp2((s_refs[slot][0:nk, cols] - m_new).astype(BF16))
                acc_ref[:, cols] = alpha * acc_ref[:, cols] + _dot(v_aug[:, 0:nk], p)
            m_ref[:, cols] = m_new
        if kb == qi:
            out = acc_ref[0:HEAD_DIM, :] / acc_ref[HEAD_DIM:HEAD_DIM + 1, :]
            o_ref[qi * tq:(qi + 1) * tq, :] = out.T.astype(o_ref.dtype)

    pairs = [(qi, kb) for qi in range(nb) for kb in range(qi + 1)]
    scores(*pairs[0], 0)
    for i, (qi, kb) in enumerate(pairs):
        if i + 1 < len(pairs):
            scores(*pairs[i + 1], (i + 1) % 2)
        accumulate(qi, kb, i % 2)


def _attn_call(q_t, k, kx, v_t, n_seq, seq_len, tq):
    d = k.shape[1]
    nh = d // HEAD_DIM
    nb = seq_len // tq
    kern = functools.partial(_attn_kernel, tq=tq, nb=nb)
    return pl.pallas_call(
        kern,
        grid=(n_seq, nh),
        in_specs=[
            pl.BlockSpec((nb, HEAD_DIM, tq), lambda n, h: (n, h, 0)),
            pl.BlockSpec((seq_len, HEAD_DIM), lambda n, h: (n, h)),
            pl.BlockSpec((seq_len, HEAD_DIM), lambda n, h: (n, h)),
            pl.BlockSpec((nb, HEAD_DIM, tq), lambda n, h: (n, h, 0)),
        ],
        out_specs=pl.BlockSpec((seq_len, HEAD_DIM), lambda n, h: (n, h)),
        out_shape=jax.ShapeDtypeStruct((n_seq * seq_len, d), BF16),
        scratch_shapes=[pltpu.VMEM((tq, tq), F32), pltpu.VMEM((tq, tq), F32),
                        pltpu.VMEM((1, tq), F32), pltpu.VMEM((1, tq), F32),
                        pltpu.VMEM((1, tq), F32),
                        pltpu.VMEM((HEAD_DIM + 2 * SUBLANES, tq), F32)],
        compiler_params=_params(2),
        name="fox_prompt",
    )(q_t, k, kx, v_t)


def _bias_kernel(pt_ref, *refs, n_steps):
    nb = BIAS_PAGES_PER_STEP
    lf_refs = refs[0:nb]
    sfx_ref, tot_ref, later_ref, spread_ref, o_ref, x_ref = refs[nb:]
    s = pl.program_id(1)
    base = pl.multiple_of(s * nb, nb)
    for e in range(nb):
        x_ref[pl.ds(base + e, 1), :] = lf_refs[e][0]

    @pl.when(s == n_steps - 1)
    def _():
        x = x_ref[...]
        within = _dot3(x, sfx_ref[...])
        tot = _dot3(x, tot_ref[...])
        later = _dot3_left(later_ref[...], tot)
        o_ref[0] = within + _dot3(later, spread_ref[...])


def _bias_call(page_table, lf_flat, sfx, tot_m, later_m, spread):
    b, n_pages = page_table.shape
    width = lf_flat.shape[2]
    nb = BIAS_PAGES_PER_STEP
    n_steps = n_pages // nb

    def page_map(e):
        return lambda bi, s, pt: (pt[bi, s * nb + e], 0, 0)

    const2 = lambda bi, s, pt: (0, 0)
    in_specs = [pl.BlockSpec((1, 1, width), page_map(e)) for e in range(nb)]
    in_specs += [pl.BlockSpec(sfx.shape, const2), pl.BlockSpec(tot_m.shape, const2),
                 pl.BlockSpec(later_m.shape, const2), pl.BlockSpec(spread.shape, const2)]
    grid_spec = pltpu.PrefetchScalarGridSpec(
        num_scalar_prefetch=1,
        grid=(b, n_steps),
        in_specs=in_specs,
        out_specs=pl.BlockSpec((1, n_pages, width), lambda bi, s, pt: (bi, 0, 0)),
        scratch_shapes=[pltpu.VMEM((n_pages, width), F32)],
    )
    return pl.pallas_call(
        functools.partial(_bias_kernel, n_steps=n_steps),
        grid_spec=grid_spec,
        out_shape=jax.ShapeDtypeStruct((b, n_pages, width), F32),
        compiler_params=_params(2),
        name="decode_bias",
    )(page_table, *([lf_flat] * nb), sfx, tot_m, later_m, spread)


def _sattn_kernel(pt_ref, q_ref, *refs, n_steps, n_grp):
    g_pages = PAGES_PER_STEP
    k_refs = refs[0:g_pages]
    v_refs = refs[g_pages:2 * g_pages]
    (bias_ref, mask_ref, kn_ref, vn_ref, bn_ref, maskn_ref, o_ref,
     m_ref, l_ref, acc_ref) = refs[2 * g_pages:]
    g = pl.program_id(1)
    keys_g = mask_ref.shape[1]
    dh = q_ref.shape[3]

    @pl.when(g == 0)
    def _():
        m_ref[...] = jnp.full(m_ref.shape, NEG_INF, F32)
        l_ref[...] = jnp.zeros(l_ref.shape, F32)
        acc_ref[...] = jnp.zeros(acc_ref.shape, F32)

    def update(gi, s, v_rows):
        m_old = m_ref[gi]
        m_new = jnp.maximum(m_old, jnp.max(s, axis=-1, keepdims=True))
        alpha = jnp.exp(m_old - m_new)
        p = jnp.exp(s - m_new)
        l_ref[gi] = alpha * l_ref[gi] + jnp.sum(p, axis=-1, keepdims=True)
        acc_ref[gi] = alpha * acc_ref[gi] + _dot(p.astype(BF16), v_rows)
        m_ref[gi] = m_new

    def group_rows(page_ref, gi):
        tiles = page_ref[0, pl.ds(gi, PAGE_SIZE, stride=n_grp), :, :]
        return tiles.reshape(keys_g, dh).astype(BF16)

    steps_per_block = SUBLANES // g_pages
    bias_rows = [bias_ref[0, pl.ds((g % steps_per_block) * g_pages + e, 1), :]
                 for e in range(g_pages)]

    def group_scores(gi):
        parts = [_dot_nt(q_ref[0, gi], group_rows(k_refs[e], gi))
                 + (mask_ref[...] + bias_rows[e][:, gi * keys_g:(gi + 1) * keys_g])
                 for e in range(g_pages)]
        return jnp.concatenate(parts, axis=1)

    scores = [group_scores(gi) for gi in range(n_grp)]
    for gi in range(n_grp):
        values = jnp.concatenate([group_rows(v_refs[e], gi) for e in range(g_pages)], axis=0)
        update(gi, scores[gi], values)

    @pl.when(g == n_steps - 1)
    def _():
        for gi in range(n_grp):
            s = _dot_nt(q_ref[0, gi], kn_ref[0, gi]) + (maskn_ref[...] + bn_ref[0, gi])
            update(gi, s, vn_ref[0, gi])
        o_ref[0] = (acc_ref[...] / l_ref[...]).astype(o_ref.dtype)


def _sattn_call(page_table, q4, cache_k, cache_v, bias, mask, k_new, v_new,
                bias_new, mask_new):
    b, n_grp, rows, dh = q4.shape
    n_pages = page_table.shape[1]
    n_steps = n_pages // PAGES_PER_STEP
    page_tiles = cache_k.shape[1]
    width = bias.shape[2]
    steps_per_block = SUBLANES // PAGES_PER_STEP

    def page_map(e):
        return lambda bi, g, pt: (pt[bi, g * PAGES_PER_STEP + e], 0, 0, 0)

    per_seq = lambda bi, g, pt: (bi, 0, 0, 0)
    const2 = lambda bi, g, pt: (0, 0)
    page_spec = lambda e: pl.BlockSpec((1, page_tiles, SUBLANES, dh), page_map(e))
    in_specs = [pl.BlockSpec((1, n_grp, rows, dh), per_seq)]
    in_specs += [page_spec(e) for e in range(PAGES_PER_STEP)]
    in_specs += [page_spec(e) for e in range(PAGES_PER_STEP)]
    in_specs += [
        pl.BlockSpec((1, SUBLANES, width), lambda bi, g, pt: (bi, g // steps_per_block, 0)),
        pl.BlockSpec(mask.shape, const2),
        pl.BlockSpec((1, n_grp, rows, dh), per_seq),
        pl.BlockSpec((1, n_grp, rows, dh), per_seq),
        pl.BlockSpec((1, n_grp, 1, rows), per_seq),
        pl.BlockSpec(mask_new.shape, const2),
    ]
    grid_spec = pltpu.PrefetchScalarGridSpec(
        num_scalar_prefetch=1,
        grid=(b, n_steps),
        in_specs=in_specs,
        out_specs=pl.BlockSpec((1, n_grp, rows, dh), per_seq),
        scratch_shapes=[pltpu.VMEM((n_grp, rows, 1), F32), pltpu.VMEM((n_grp, rows, 1), F32),
                        pltpu.VMEM((n_grp, rows, dh), F32)],
    )
    kern = functools.partial(_sattn_kernel, n_steps=n_steps, n_grp=n_grp)
    return pl.pallas_call(
        kern,
        grid_spec=grid_spec,
        out_shape=jax.ShapeDtypeStruct((b, n_grp, rows, dh), BF16),
        compiler_params=_params(2),
        name="fox_decode",
    )(page_table, q4, *([cache_k] * PAGES_PER_STEP), *([cache_v] * PAGES_PER_STEP),
      bias, mask, k_new, v_new, bias_new, mask_new)


def _oproj_kernel(a_ref, w_ref, h_ref, o_ref):
    o_ref[...] = h_ref[...] + _dot(a_ref[...], w_ref[...])


def _oproj_call(a, w_o, h, tm, tn):
    m, d = h.shape
    return pl.pallas_call(
        _oproj_kernel,
        grid=(m // tm, d // tn),
        in_specs=[
            pl.BlockSpec((tm, d), lambda i, j: (i, 0)),
            pl.BlockSpec((d, tn), lambda i, j: (0, j)),
            pl.BlockSpec((tm, tn), lambda i, j: (i, j)),
        ],
        out_specs=pl.BlockSpec((tm, tn), lambda i, j: (i, j)),
        out_shape=jax.ShapeDtypeStruct((m, d), F32),
        compiler_params=_params(2),
        name="o_proj",
    )(a, w_o, h)


def _cumsum_matrix(n, group):
    t = np.arange(n)[:, None]
    tp = np.arange(n)[None, :]
    return ((tp <= t) & (tp // group == t // group)).astype(BF16)


def _placement(nh, d):
    lane = np.arange(LANES)[None, :, None]
    col = np.arange(d)[None, None, :]
    piece = np.arange(N_SPLIT)[:, None, None]
    return ((lane < nh) & (col == lane * HEAD_DIM + piece)).astype(BF16)


def kernel(x_prompt, x_sample, state_conv_a, state_ffn, cache_k, cache_v, cache_logf, page_table, p_prompt, p_sample, g_mix_a, w_pw1, w_dwa, b_dwa, ln_g, ln_b, w_pw2, g_kv, w_k, w_v, g_k, w_f, b_f, g_mix_b, w_q, g_q, w_o, g_ffn, w_up, w_gate, w_fdw, b_fdw, w_down, g_ple, w_ple_gate, w_ple_proj):
    n_p, t_p, d = x_prompt.shape
    n_s, t_s, _ = x_sample.shape
    depth = g_ffn.shape[0]
    n_a = g_mix_a.shape[0]
    f = w_up.shape[2]
    nh = d // HEAD_DIM
    m_s = n_s * t_s

    bf = lambda w: w.astype(BF16)
    layers = lambda w: [bf(w[i]) for i in range(w.shape[0])]
    w_pw1_b, w_pw2_b = layers(w_pw1), layers(w_pw2)
    w_k_b, w_v_b, w_q_b, w_o_b = bf(w_k), bf(w_v), layers(w_q), layers(w_o)
    w_up_b, w_gate_b, w_down_b = bf(w_up), bf(w_gate), bf(w_down)
    w_pg_b, w_pp_b = bf(w_ple_gate), bf(w_ple_proj)
    w_f_pad = jnp.pad(bf(w_f), ((0, 0), (0, LANES - nh)))
    b_f_pad = jnp.pad(b_f, (0, LANES - nh)).reshape(1, LANES)
    place = _placement(nh, d)
    row = lambda a: a.reshape(1, -1)

    def trunk(x, p, n_seq, seq_len, a_prev32, tiles, ffn, attend, tri, transposed):
        h = x.reshape(n_seq * seq_len, d)
        a_us, f_bufs = [], []
        k = v = lf = None
        for i in range(depth):
            attn = None
            if i < n_a:
                u = _glu_call(h, row(g_mix_a[i]), w_pw1_b[i], tiles.rows, tiles.cols)
                h = _conva_call(u, a_prev32[i], w_dwa[i], row(b_dwa[i]), row(ln_g[i]),
                                row(ln_b[i]), w_pw2_b[i], h, tiles.rows_conv, seq_len)
                a_us.append(u.reshape(n_seq, seq_len, d))
            else:
                jb = i - n_a
                outs = _qkv_call(
                    h, row(g_kv), row(g_mix_b[jb]), w_k_b, w_v_b, w_q_b[jb],
                    row(g_k), row(g_q[jb]), w_f_pad, b_f_pad, tri, place,
                    tiles.rows_qkv, tiles.cols, seq_len, transposed)
                k, v = outs[0], outs[1]
                lf = outs[6] if transposed else outs[5]
                attn = (attend(outs), w_o_b[jb])
            h, fb = ffn(i, h, attn)
            f_bufs.append(fb)
            h = _ple_call(h, row(g_ple[i]), p[i].reshape(n_seq * seq_len, -1),
                          w_pg_b, w_pp_b, tiles.rows_ple, i)
        shp = (n_seq, seq_len, nh, HEAD_DIM)
        return (h.reshape(n_seq, seq_len, d), a_us, jnp.stack(f_bufs),
                k.reshape(shp), v.reshape(shp),
                lf[:, :nh].reshape(n_seq, seq_len, nh))

    tp = PROMPT_TILES

    def ffn_prompt(i, h, attn):
        prev = jnp.zeros((n_p, CONV_F_WIDTH - 1, f), F32)
        return _ffn_seq_call(h, row(g_ffn[i]), w_up_b, w_gate_b, w_fdw[i],
                             row(b_fdw[i]), w_down_b, prev, tp.rows_ffn, tp.cols, t_p, i,
                             attn)

    def attend_prompt(outs):
        _, _, kb, kx, v_t, q_t, _ = outs
        return _attn_call(q_t, kb, kx, v_t, n_p, t_p, tp.rows_qkv)

    a_prev_p = [jnp.zeros((n_p, HALO_ROWS, d), F32) for _ in range(n_a)]
    (y_p, a_us_p, ffn_p, k_p, v_p, lf_p) = trunk(
        x_prompt, p_prompt, n_p, t_p, a_prev_p, tp, ffn_prompt,
        attend_prompt, _cumsum_matrix(tp.rows_qkv, tp.rows_qkv), True)
    conv_a_p = jnp.stack([u[:, t_p - (CONV_A_WIDTH - 1):] for u in a_us_p])

    def ffn_sample(i, h, attn):
        if attn is not None:
            h = _oproj_call(attn[0], attn[1], h, m_s, tp.cols)
        st = state_ffn[i]
        z = jnp.zeros((n_s, t_s, f), F32)
        p1 = z.at[:, 0].set(st[:, 1]).reshape(m_s, f)
        p2 = z.at[:, 0].set(st[:, 0]).at[:, 1].set(st[:, 1]).reshape(m_s, f)
        h, u = _ffn_multi_call(h, row(g_ffn[i]), w_up_b, w_gate_b, w_fdw[i],
                               row(b_fdw[i]), w_down_b, p1, p2, tp.cols, t_s, i)
        full = jnp.concatenate([st, u.reshape(n_s, t_s, f)], axis=1)
        return h, full[:, -(CONV_F_WIDTH - 1):]

    n_pool = cache_k.shape[0]
    n_pages = page_table.shape[1]
    n_grp = nh // SUBLANES
    keys_g = PAGE_SIZE * SUBLANES
    page_rows = PAGE_SIZE * nh
    tiles_shape = (n_pool, PAGE_SIZE * n_grp, SUBLANES, HEAD_DIM)
    cache_k4 = cache_k.reshape(tiles_shape)
    cache_v4 = cache_v.reshape(tiles_shape)
    lf_flat = cache_logf.reshape(n_pool, 1, page_rows)
    fr = np.arange(page_rows)
    t_in, h_in = fr // nh, fr % nh
    t_out = (fr % keys_g) // SUBLANES
    h_out = (fr // keys_g) * SUBLANES + fr % SUBLANES
    lane = np.arange(LANES)
    sfx = ((h_in[:, None] == h_out[None, :]) & (t_in[:, None] > t_out[None, :])).astype(BF16)
    tot_m = (h_in[:, None] == lane[None, :]).astype(BF16)
    pg = np.arange(n_pages)
    later_m = (pg[None, :] > pg[:, None]).astype(BF16)
    spread = (lane[:, None] == h_out[None, :]).astype(BF16)
    qrow = np.arange(t_s * SUBLANES)
    kcol = np.arange(keys_g)
    mask = np.where((qrow[:, None] % SUBLANES) == (kcol[None, :] % SUBLANES),
                    0.0, NEG_INF).astype(np.float32)
    mask_new = np.where(((qrow[:, None] % SUBLANES) == (qrow[None, :] % SUBLANES))
                        & (qrow[None, :] // SUBLANES <= qrow[:, None] // SUBLANES),
                        0.0, NEG_INF).astype(np.float32)
    bias = _bias_call(page_table, lf_flat, sfx, tot_m, later_m, spread)

    def by_group(a, width):
        a5 = a.reshape(n_s, t_s, n_grp, SUBLANES, width)
        return jnp.transpose(a5, (0, 2, 1, 3, 4)).reshape(n_s, n_grp, t_s * SUBLANES, width)

    def attend_sample(outs):
        _, _, kb, vb, q, _, c = outs
        bias_new = -by_group(c[:, :nh], 1).reshape(n_s, n_grp, 1, t_s * SUBLANES)
        a = _sattn_call(page_table, by_group(q, HEAD_DIM), cache_k4, cache_v4, bias, mask,
                        by_group(kb, HEAD_DIM), by_group(vb, HEAD_DIM), bias_new, mask_new)
        a5 = a.reshape(n_s, n_grp, t_s, SUBLANES, HEAD_DIM)
        return jnp.transpose(a5, (0, 2, 1, 3, 4)).reshape(m_s, d)

    a_prev_s = [jnp.pad(state_conv_a[i], ((0, 0), (HALO_ROWS - (CONV_A_WIDTH - 1), 0), (0, 0)))
                for i in range(n_a)]
    ts = _Tiles(rows=m_s, rows_qkv=m_s, rows_ffn=m_s, rows_conv=t_s, rows_ple=m_s,
                cols=tp.cols)
    (y_s, a_us_s, ffn_s, k_s, v_s, lf_s) = trunk(
        x_sample, p_sample, n_s, t_s, a_prev_s, ts, ffn_sample,
        attend_sample, _cumsum_matrix(m_s, t_s), False)
    conv_a_s = jnp.stack([
        jnp.concatenate([state_conv_a[i], a_us_s[i]], axis=1)[:, -(CONV_A_WIDTH - 1):]
        for i in range(n_a)])

    return (y_p, y_s, conv_a_p, ffn_p, k_p, v_p, lf_p,
            conv_a_s, ffn_s, k_s, v_s, lf_s)
```

```python
import functools
from typing import NamedTuple

import jax
import jax.numpy as jnp
import numpy as np
from jax import lax
from jax.experimental import pallas as pl
from jax.experimental.pallas import tpu as pltpu

F32 = jnp.float32
BF16 = jnp.bfloat16

HEAD_DIM = 128
CONV_A_WIDTH = 31
CONV_F_WIDTH = 3
PAGE_SIZE = 128
EPS = 1e-6
NEG_INF = -1e30
LOG2_E = 1.4426950408889634

LANES = 128
SUBLANES = 8
HALO_ROWS = 32
VMEM_LIMIT = 56 * 1024 * 1024
PAGES_PER_STEP = 4
RING_DEPTH = 4
BIAS_PAGES_PER_STEP = 32
N_SPLIT = 3
MXU_COLS = 256


class _Tiles(NamedTuple):
    rows: int
    rows_qkv: int
    rows_ffn: int
    rows_conv: int
    rows_ple: int
    cols: int


PROMPT_TILES = _Tiles(rows=1024, rows_qkv=512, rows_ffn=512, rows_conv=256,
                      rows_ple=512, cols=512)


def _params(n_axes):
    return pltpu.CompilerParams(
        dimension_semantics=("arbitrary",) * n_axes,
        vmem_limit_bytes=VMEM_LIMIT)


def _dot(a, b):
    return jnp.dot(a, b, preferred_element_type=F32)


def _dot_nt(a, b):
    return lax.dot_general(a, b, (((1,), (1,)), ((), ())),
                           preferred_element_type=F32)


def _sigmoid(x):
    return jax.nn.sigmoid(x)


def _log_sigmoid(z):
    return -(jnp.maximum(-z, 0.0) + jnp.log1p(jnp.exp(-jnp.abs(z))))


def _rms_rows(x):
    ms = jnp.mean(x * x, axis=-1, keepdims=True)
    return x * lax.rsqrt(ms + EPS)


def _split3(x):
    hi = x.astype(BF16)
    r1 = x - hi.astype(F32)
    mid = r1.astype(BF16)
    lo = (r1 - mid.astype(F32)).astype(BF16)
    return hi, mid, lo


def _dot3(x, w_bf16):
    hi, mid, lo = _split3(x)
    return _dot(hi, w_bf16) + _dot(mid, w_bf16) + _dot(lo, w_bf16)


def _dot3_left(w_bf16, x):
    hi, mid, lo = _split3(x)
    return _dot(w_bf16, hi) + _dot(w_bf16, mid) + _dot(w_bf16, lo)


def _glu_kernel(x_ref, g_ref, w1_ref, w2_ref, u_ref, xn_ref):
    @pl.when(pl.program_id(1) == 0)
    def _():
        xn_ref[...] = (_rms_rows(x_ref[...]) * g_ref[...]).astype(BF16)

    xn = xn_ref[...]
    for c0 in range(0, u_ref.shape[1], MXU_COLS):
        cols = slice(c0, c0 + MXU_COLS)
        a1 = _dot(xn, w1_ref[:, cols])
        a2 = _dot(xn, w2_ref[:, cols])
        u_ref[:, cols] = a1 * _sigmoid(a2)


def _glu_call(h, g, w_pw1, tm, tn):
    m, d = h.shape
    nj = d // tn
    return pl.pallas_call(
        _glu_kernel,
        grid=(m // tm, nj),
        in_specs=[
            pl.BlockSpec((tm, d), lambda i, j: (i, 0)),
            pl.BlockSpec((1, d), lambda i, j: (0, 0)),
            pl.BlockSpec((d, tn), lambda i, j: (0, j)),
            pl.BlockSpec((d, tn), lambda i, j: (0, j + nj)),
        ],
        out_specs=pl.BlockSpec((tm, tn), lambda i, j: (i, j)),
        out_shape=jax.ShapeDtypeStruct((m, d), F32),
        scratch_shapes=[pltpu.VMEM((tm, d), BF16)],
        compiler_params=_params(2),
        name="glu",
    )(h, g, w_pw1, w_pw1)


def _conva_kernel(u_ref, halo_ref, prev_ref, wdw_ref, bdw_ref, lng_ref, lnb_ref,
                  w2_ref, h_ref, o_ref, full_ref, c_ref, *, tm, tps, rb):
    i = pl.program_id(0)
    d = u_ref.shape[1]
    first = (i % tps) == 0
    full_ref[0:HALO_ROWS, :] = jnp.where(first, prev_ref[0], halo_ref[...])
    full_ref[HALO_ROWS:HALO_ROWS + tm, :] = u_ref[...]
    full_ref[HALO_ROWS + tm:HALO_ROWS + tm + SUBLANES, :] = jnp.zeros((SUBLANES, d), F32)
    off = HALO_ROWS - (CONV_A_WIDTH - 1)

    def conv_chunk(r0, c0):
        cols = slice(c0, c0 + LANES)
        acc = jnp.broadcast_to(bdw_ref[:, cols], (rb, LANES))
        for b in range(SUBLANES):
            z = None
            for j in range(CONV_A_WIDTH):
                if (off + j) % SUBLANES != b:
                    continue
                s = r0 + (off + j) // SUBLANES * SUBLANES
                term = wdw_ref[j:j + 1, cols] * full_ref[s:s + rb + SUBLANES, cols]
                z = term if z is None else z + term
            acc = acc + z[b:b + rb, :]
        c_ref[r0:r0 + rb, cols] = acc

    for r0 in range(0, tm, rb):
        for c0 in range(0, d, LANES):
            conv_chunk(r0, c0)
    c = c_ref[...]
    mu = jnp.mean(c, axis=-1, keepdims=True)
    xc = c - mu
    var = jnp.mean(xc * xc, axis=-1, keepdims=True)
    y = xc * lax.rsqrt(var + EPS) * lng_ref[...] + lnb_ref[...]
    a = (y * _sigmoid(y)).astype(BF16)
    o_ref[...] = h_ref[...] + _dot(a, w2_ref[...])


def _conva_call(u, prev32, w_dw, b_dw, ln_g, ln_b, w_pw2, h, tm, seq_len):
    m, d = u.shape
    tps = max(seq_len // tm, 1)
    rb = min(tm, 128)
    if tm % HALO_ROWS == 0:
        k = tm // HALO_ROWS
        halo_map = lambda i: (jnp.maximum(i * k - 1, 0), 0)
    else:
        assert tps == 1
        halo_map = lambda i: (0, 0)
    kern = functools.partial(_conva_kernel, tm=tm, tps=tps, rb=rb)
    return pl.pallas_call(
        kern,
        grid=(m // tm,),
        in_specs=[
            pl.BlockSpec((tm, d), lambda i: (i, 0)),
            pl.BlockSpec((HALO_ROWS, d), halo_map),
            pl.BlockSpec((1, HALO_ROWS, d), lambda i: (i // tps, 0, 0)),
            pl.BlockSpec((CONV_A_WIDTH, d), lambda i: (0, 0)),
            pl.BlockSpec((1, d), lambda i: (0, 0)),
            pl.BlockSpec((1, d), lambda i: (0, 0)),
            pl.BlockSpec((1, d), lambda i: (0, 0)),
            pl.BlockSpec((d, d), lambda i: (0, 0), pipeline_mode=pl.Buffered(1)),
            pl.BlockSpec((tm, d), lambda i: (i, 0)),
        ],
        out_specs=pl.BlockSpec((tm, d), lambda i: (i, 0)),
        out_shape=jax.ShapeDtypeStruct((m, d), F32),
        scratch_shapes=[pltpu.VMEM((HALO_ROWS + tm + SUBLANES, d), F32),
                        pltpu.VMEM((tm, d), F32)],
        compiler_params=_params(1),
        name="conv_a",
    )(u, u, prev32, w_dw, b_dw, ln_g, ln_b, w_pw2, h)


def _ffn_body(h_ref, g_ref, wu_ref, wg_ref, wdw_ref, bdw_ref, wd_ref,
              o_ref, xn_ref, ubuf_ref, tm, prev_rows, fix_shifted):
    xn = xn_ref[...]
    u = _dot(xn, wu_ref[...])
    ubuf_ref[SUBLANES:SUBLANES + tm, :] = u
    ubuf_ref[SUBLANES - 2:SUBLANES, :] = prev_rows
    u1 = ubuf_ref[SUBLANES - 1:SUBLANES - 1 + tm, :]
    u2 = ubuf_ref[SUBLANES - 2:SUBLANES - 2 + tm, :]
    u1, u2 = fix_shifted(u1, u2)
    c = (wdw_ref[0:1, :] * u2 + wdw_ref[1:2, :] * u1 + wdw_ref[2:3, :] * u
         + bdw_ref[...])
    gt = _dot(xn, wg_ref[...])
    act = (c * _sigmoid(c) * gt).astype(BF16)
    o_ref[...] += _dot(act, wd_ref[...])


def _ffn_seq_kernel(h_ref, g_ref, wu_ref, wg_ref, wdw_ref, bdw_ref, wd_ref,
                    *rest, tm, tps, with_attn):
    if with_attn:
        a_ref, wo_ref, *rest = rest
    prev_ref, o_ref, fb_ref, xn_ref, ubuf_ref, carry_ref = rest
    i = pl.program_id(0)
    j = pl.program_id(1)
    first = (i % tps) == 0

    @pl.when(j == 0)
    def _():
        h = h_ref[...]
        if with_attn:
            h = h + _dot(a_ref[...], wo_ref[...])
        xn_ref[...] = (_rms_rows(h) * g_ref[...]).astype(BF16)
        o_ref[...] = h

        @pl.when(i == 0)
        def _():
            carry_ref[...] = jnp.zeros(carry_ref.shape, F32)

    prev_rows = jnp.where(first, prev_ref[0], carry_ref[j])
    _ffn_body(h_ref, g_ref, wu_ref, wg_ref, wdw_ref, bdw_ref, wd_ref,
              o_ref, xn_ref, ubuf_ref, tm, prev_rows, lambda a, b: (a, b))
    tail = ubuf_ref[SUBLANES + tm - 2:SUBLANES + tm, :]
    carry_ref[j] = tail

    @pl.when((i % tps) == tps - 1)
    def _():
        fb_ref[i // tps, j] = tail


def _ffn_multi_kernel(h_ref, g_ref, wu_ref, wg_ref, wdw_ref, bdw_ref, wd_ref,
                      p1_ref, p2_ref, o_ref, u_ref, xn_ref, ubuf_ref,
                      *, tm, seq_len):
    @pl.when(pl.program_id(1) == 0)
    def _():
        xn_ref[...] = (_rms_rows(h_ref[...]) * g_ref[...]).astype(BF16)
        o_ref[...] = h_ref[...]
        ubuf_ref[0:SUBLANES, :] = jnp.zeros((SUBLANES, ubuf_ref.shape[1]), F32)

    def fix_shifted(u1, u2):
        t = lax.broadcasted_iota(jnp.int32, (tm, 1), 0) % seq_len
        return (jnp.where(t >= 1, u1, p1_ref[...]),
                jnp.where(t >= 2, u2, p2_ref[...]))

    prev_rows = jnp.zeros((CONV_F_WIDTH - 1, ubuf_ref.shape[1]), F32)
    _ffn_body(h_ref, g_ref, wu_ref, wg_ref, wdw_ref, bdw_ref, wd_ref,
              o_ref, xn_ref, ubuf_ref, tm, prev_rows, fix_shifted)
    u_ref[...] = ubuf_ref[SUBLANES:SUBLANES + tm, :]


def _ffn_common_specs(tm, tn, d, f, layer):
    return [
        pl.BlockSpec((tm, d), lambda i, j: (i, 0)),
        pl.BlockSpec((1, d), lambda i, j: (0, 0)),
        pl.BlockSpec((None, d, tn), lambda i, j: (layer, 0, j)),
        pl.BlockSpec((None, d, tn), lambda i, j: (layer, 0, j)),
        pl.BlockSpec((CONV_F_WIDTH, tn), lambda i, j: (0, j)),
        pl.BlockSpec((1, tn), lambda i, j: (0, j)),
        pl.BlockSpec((None, tn, d), lambda i, j: (layer, j, 0)),
    ]


def _ffn_seq_call(h, g, w_up, w_gate, w_dw, b_dw, w_down, prev, tm, tn, seq_len, layer,
                  attn=None):
    m, d = h.shape
    f = w_up.shape[2]
    n_seq = m // seq_len
    tps = seq_len // tm
    nj = f // tn
    kern = functools.partial(_ffn_seq_kernel, tm=tm, tps=tps, with_attn=attn is not None)
    attn_specs = [] if attn is None else [
        pl.BlockSpec((tm, d), lambda i, j: (i, 0)),
        pl.BlockSpec((d, d), lambda i, j: (0, 0), pipeline_mode=pl.Buffered(1)),
    ]
    h_new, fb = pl.pallas_call(
        kern,
        grid=(m // tm, nj),
        in_specs=_ffn_common_specs(tm, tn, d, f, layer) + attn_specs + [
            pl.BlockSpec((1, CONV_F_WIDTH - 1, tn), lambda i, j: (i // tps, 0, j)),
        ],
        out_specs=[
            pl.BlockSpec((tm, d), lambda i, j: (i, 0)),
            pl.BlockSpec((n_seq, nj, CONV_F_WIDTH - 1, tn), lambda i, j: (0, 0, 0, 0)),
        ],
        out_shape=[jax.ShapeDtypeStruct((m, d), F32),
                   jax.ShapeDtypeStruct((n_seq, nj, CONV_F_WIDTH - 1, tn), F32)],
        scratch_shapes=[pltpu.VMEM((tm, d), BF16),
                        pltpu.VMEM((SUBLANES + tm, tn), F32),
                        pltpu.VMEM((nj, CONV_F_WIDTH - 1, tn), F32)],
        compiler_params=_params(2),
        name="ffn_seq",
    )(h, g, w_up, w_gate, w_dw, b_dw, w_down, *(attn or ()), prev)
    fb = jnp.transpose(fb, (0, 2, 1, 3)).reshape(n_seq, CONV_F_WIDTH - 1, f)
    return h_new, fb


def _ffn_multi_call(h, g, w_up, w_gate, w_dw, b_dw, w_down, p1, p2, tn, seq_len, layer):
    m, d = h.shape
    f = w_up.shape[2]
    kern = functools.partial(_ffn_multi_kernel, tm=m, seq_len=seq_len)
    return pl.pallas_call(
        kern,
        grid=(1, f // tn),
        in_specs=_ffn_common_specs(m, tn, d, f, layer) + [
            pl.BlockSpec((m, tn), lambda i, j: (0, j)),
            pl.BlockSpec((m, tn), lambda i, j: (0, j)),
        ],
        out_specs=[
            pl.BlockSpec((m, d), lambda i, j: (0, 0)),
            pl.BlockSpec((m, tn), lambda i, j: (0, j)),
        ],
        out_shape=[jax.ShapeDtypeStruct((m, d), F32),
                   jax.ShapeDtypeStruct((m, f), F32)],
        scratch_shapes=[pltpu.VMEM((m, d), BF16),
                        pltpu.VMEM((SUBLANES + m, tn), F32)],
        compiler_params=_params(2),
        name="ffn_multi",
    )(h, g, w_up, w_gate, w_dw, b_dw, w_down, p1, p2)


def _ple_kernel(h_ref, g_ref, p_ref, wg_ref, wp_ref, o_ref):
    h = h_ref[...]
    xn = (_rms_rows(h) * g_ref[...]).astype(BF16)
    gate = _dot(xn, wg_ref[...])
    proj = _dot(p_ref[...].astype(BF16), wp_ref[...])
    o_ref[...] = h + _sigmoid(gate) * proj


def _ple_call(h, g, p, w_gate, w_proj, tm, layer):
    m, d = h.shape
    pd = p.shape[1]
    resident = pl.Buffered(1)
    return pl.pallas_call(
        _ple_kernel,
        grid=(m // tm,),
        in_specs=[
            pl.BlockSpec((tm, d), lambda i: (i, 0)),
            pl.BlockSpec((1, d), lambda i: (0, 0)),
            pl.BlockSpec((tm, pd), lambda i: (i, 0)),
            pl.BlockSpec((None, d, d), lambda i: (layer, 0, 0), pipeline_mode=resident),
            pl.BlockSpec((None, pd, d), lambda i: (layer, 0, 0), pipeline_mode=resident),
        ],
        out_specs=pl.BlockSpec((tm, d), lambda i: (i, 0)),
        out_shape=jax.ShapeDtypeStruct((m, d), F32),
        compiler_params=_params(1),
        name="ple",
    )(h, g, p, w_gate, w_proj)


def _qkv_kernel(h_ref, gkv_ref, gmix_ref, wk_ref, wv_ref, wq_ref, gk_ref, gq_ref,
                wf_ref, bf_ref, tri_ref, place_ref, *refs, tm, tn, tps, transposed):
    if transposed:
        (k_ref, v_ref, kb_ref, kx_ref, vt_ref, qt_ref, lf_ref,
         xkv_ref, xq_ref, carry_ref, cp_ref) = refs
    else:
        (k_ref, v_ref, kb_ref, vb_ref, q_ref, lf_ref, c_ref,
         xkv_ref, xq_ref, carry_ref) = refs
    i = pl.program_id(0)
    j = pl.program_id(1)

    @pl.when(j == 0)
    def _():
        r = _rms_rows(h_ref[...])
        xkv = (r * gkv_ref[...]).astype(BF16)
        xkv_ref[...] = xkv
        xq_ref[...] = (r * gmix_ref[...]).astype(BF16)
        lf = _log_sigmoid(_dot(xkv, wf_ref[...]) + bf_ref[...])
        lf_ref[...] = lf

        @pl.when((i % tps) == 0)
        def _():
            carry_ref[...] = jnp.zeros(carry_ref.shape, F32)

        c = _dot3_left(tri_ref[...], lf) + carry_ref[...]
        carry_ref[...] = c[tm - 1:tm, :]
        if transposed:
            for piece, val in enumerate(_split3(-LOG2_E * c)):
                cp_ref[piece] = val
        else:
            c_ref[...] = c

    xkv = xkv_ref[...]
    kk = _dot(xkv, wk_ref[...])
    vv = _dot(xkv, wv_ref[...])
    qq = _dot(xq_ref[...], wq_ref[...])
    v_ref[...] = vv
    scale = HEAD_DIM ** -0.5 * (LOG2_E if transposed else 1.0)
    qn_heads = []
    for c0 in range(0, tn, HEAD_DIM):
        sl = slice(c0, c0 + HEAD_DIM)
        kn = _rms_rows(kk[:, sl]) * gk_ref[...]
        k_ref[:, sl] = kn
        kb_ref[:, sl] = kn.astype(BF16)
        qn_heads.append(_rms_rows(qq[:, sl]) * gq_ref[...] * scale)
    qn = jnp.concatenate(qn_heads, axis=1)
    if transposed:
        vt_ref[0] = vv.T.astype(BF16)
        qt_ref[0] = qn.T.astype(BF16)
        kx = _dot(cp_ref[0], place_ref[0])
        for piece in range(1, N_SPLIT):
            kx = kx + _dot(cp_ref[piece], place_ref[piece])
        kx_ref[...] = kx.astype(BF16)
    else:
        vb_ref[...] = vv.astype(BF16)
        q_ref[...] = qn.astype(BF16)


def _qkv_call(h, g_kv, g_mix, w_k, w_v, w_q, g_k, g_q, w_f_pad, b_f_pad,
              tri, place, tm, tn, seq_len, transposed):
    m, d = h.shape
    tps = max(seq_len // tm, 1)
    kern = functools.partial(_qkv_kernel, tm=tm, tn=tn, tps=tps, transposed=transposed)
    row = lambda i, j: (i, 0)
    const = lambda i, j: (0, 0)
    col = lambda i, j: (0, j)
    tile = lambda i, j: (i, j)
    tile_t = lambda i, j: (i, j, 0)
    in_specs = [
        pl.BlockSpec((tm, d), row),
        pl.BlockSpec((1, d), const),
        pl.BlockSpec((1, d), const),
        pl.BlockSpec((d, tn), col),
        pl.BlockSpec((d, tn), col),
        pl.BlockSpec((d, tn), col),
        pl.BlockSpec((1, HEAD_DIM), const),
        pl.BlockSpec((1, HEAD_DIM), const),
        pl.BlockSpec((d, LANES), const),
        pl.BlockSpec((1, LANES), const),
        pl.BlockSpec((tm, tm), const),
        pl.BlockSpec((N_SPLIT, LANES, tn), lambda i, j: (0, 0, j)),
    ]
    scratch = [pltpu.VMEM((tm, d), BF16), pltpu.VMEM((tm, d), BF16),
               pltpu.VMEM((1, LANES), F32)]
    f32_md = jax.ShapeDtypeStruct((m, d), F32)
    bf_md = jax.ShapeDtypeStruct((m, d), BF16)
    if transposed:
        out_specs = [
            pl.BlockSpec((tm, tn), tile), pl.BlockSpec((tm, tn), tile),
            pl.BlockSpec((tm, tn), tile), pl.BlockSpec((tm, tn), tile),
            pl.BlockSpec((1, tn, tm), tile_t), pl.BlockSpec((1, tn, tm), tile_t),
            pl.BlockSpec((tm, LANES), row),
        ]
        bf_t = jax.ShapeDtypeStruct((m // tm, d, tm), BF16)
        out_shape = [f32_md, f32_md, bf_md, bf_md, bf_t, bf_t,
                     jax.ShapeDtypeStruct((m, LANES), F32)]
        scratch.append(pltpu.VMEM((N_SPLIT, tm, LANES), BF16))
    else:
        out_specs = [
            pl.BlockSpec((tm, tn), tile), pl.BlockSpec((tm, tn), tile),
            pl.BlockSpec((tm, tn), tile), pl.BlockSpec((tm, tn), tile),
            pl.BlockSpec((tm, tn), tile),
            pl.BlockSpec((tm, LANES), row), pl.BlockSpec((tm, LANES), row),
        ]
        out_shape = [f32_md, f32_md, bf_md, bf_md, bf_md,
                     jax.ShapeDtypeStruct((m, LANES), F32),
                     jax.ShapeDtypeStruct((m, LANES), F32)]
    return pl.pallas_call(
        kern,
        grid=(m // tm, d // tn),
        in_specs=in_specs,
        out_specs=out_specs,
        out_shape=out_shape,
        scratch_shapes=scratch,
        compiler_params=_params(2),
        name="qkv_t" if transposed else "qkv",
    )(h, g_kv, g_mix, w_k, w_v, w_q, g_k, g_q, w_f_pad, b_f_pad, tri, place)


def _attn_kernel(q_ref, k_ref, kx_ref, v_ref, o_ref,
                 s0_ref, s1_ref, bm0_ref, bm1_ref, m_ref, acc_ref, *, tq, nb):
    ones_rows = (lax.broadcasted_iota(jnp.int32, (HEAD_DIM, tq), 0) < N_SPLIT).astype(BF16)
    sum_rows = jnp.ones((acc_ref.shape[0] - HEAD_DIM, tq), BF16)
    s_refs = (s0_ref, s1_ref)
    bm_refs = (bm0_ref, bm1_ref)

    def n_keys(qi, kb, c0):
        return min(tq, c0 + MXU_COLS) if kb == qi else tq

    def scores(qi, kb, slot):
        q_aug = jnp.concatenate([q_ref[qi], ones_rows], axis=0)
        rows = slice(kb * tq, (kb + 1) * tq)
        k_aug = jnp.concatenate([k_ref[rows, :], kx_ref[rows, :]], axis=1)
        for c0 in range(0, tq, MXU_COLS):
            cols = slice(c0, c0 + MXU_COLS)
            nk = n_keys(qi, kb, c0)
            s = _dot(k_aug[0:nk, :], q_aug[:, cols])
            if kb == qi:
                key = lax.broadcasted_iota(jnp.int32, s.shape, 0)
                qry = lax.broadcasted_iota(jnp.int32, s.shape, 1) + c0
                s = jnp.where(key <= qry, s, NEG_INF)
            s_refs[slot][0:nk, cols] = s
            bm_refs[slot][:, cols] = jnp.max(s, axis=0, keepdims=True)

    def accumulate(qi, kb, slot):
        v_aug = jnp.concatenate([v_ref[kb], sum_rows], axis=0)
        for c0 in range(0, tq, MXU_COLS):
            cols = slice(c0, c0 + MXU_COLS)
            nk = n_keys(qi, kb, c0)
            if kb == 0:
                m_new = bm_refs[slot][:, cols]
                p = jnp.exp2((s_refs[slot][0:nk, cols] - m_new).astype(BF16))
                acc_ref[:, cols] = _dot(v_aug[:, 0:nk], p)
            else:
                m_old = m_ref[:, cols]
                m_new = jnp.maximum(m_old, bm_refs[slot][:, cols])
                alpha = jnp.exp2(m_old - m_new)
                p = jnp.exp2((s_refs[slot][0:nk, cols] - m_new).astype(BF16))
                acc_ref[:, cols] = alpha * acc_ref[:, cols] + _dot(v_aug[:, 0:nk], p)
            m_ref[:, cols] = m_new
        if kb == qi:
            out = acc_ref[0:HEAD_DIM, :] / acc_ref[HEAD_DIM:HEAD_DIM + 1, :]
            o_ref[qi * tq:(qi + 1) * tq, :] = out.T.astype(o_ref.dtype)

    pairs = [(qi, kb) for qi in range(nb) for kb in range(qi + 1)]
    scores(*pairs[0], 0)
    for i, (qi, kb) in enumerate(pairs):
        if i + 1 < len(pairs):
            scores(*pairs[i + 1], (i + 1) % 2)
        accumulate(qi, kb, i % 2)


def _attn_call(q_t, k, kx, v_t, n_seq, seq_len, tq):
    d = k.shape[1]
    nh = d // HEAD_DIM
    nb = seq_len // tq
    kern = functools.partial(_attn_kernel, tq=tq, nb=nb)
    return pl.pallas_call(
        kern,
        grid=(n_seq, nh),
        in_specs=[
            pl.BlockSpec((nb, HEAD_DIM, tq), lambda n, h: (n, h, 0)),
            pl.BlockSpec((seq_len, HEAD_DIM), lambda n, h: (n, h)),
            pl.BlockSpec((seq_len, HEAD_DIM), lambda n, h: (n, h)),
            pl.BlockSpec((nb, HEAD_DIM, tq), lambda n, h: (n, h, 0)),
        ],
        out_specs=pl.BlockSpec((seq_len, HEAD_DIM), lambda n, h: (n, h)),
        out_shape=jax.ShapeDtypeStruct((n_seq * seq_len, d), BF16),
        scratch_shapes=[pltpu.VMEM((tq, tq), F32), pltpu.VMEM((tq, tq), F32),
                        pltpu.VMEM((1, tq), F32), pltpu.VMEM((1, tq), F32),
                        pltpu.VMEM((1, tq), F32),
                        pltpu.VMEM((HEAD_DIM + 2 * SUBLANES, tq), F32)],
        compiler_params=_params(2),
        name="fox_prompt",
    )(q_t, k, kx, v_t)


def _bias_kernel(pt_ref, *refs, n_steps):
    nb = BIAS_PAGES_PER_STEP
    lf_refs = refs[0:nb]
    sfx_ref, tot_ref, later_ref, spread_ref, o_ref, x_ref = refs[nb:]
    s = pl.program_id(1)
    base = pl.multiple_of(s * nb, nb)
    for e in range(nb):
        x_ref[pl.ds(base + e, 1), :] = lf_refs[e][0]

    @pl.when(s == n_steps - 1)
    def _():
        x = x_ref[...]
        within = _dot3(x, sfx_ref[...])
        tot = _dot3(x, tot_ref[...])
        later = _dot3_left(later_ref[...], tot)
        o_ref[0] = within + _dot3(later, spread_ref[...])


def _bias_call(page_table, lf_flat, sfx, tot_m, later_m, spread):
    b, n_pages = page_table.shape
    width = lf_flat.shape[2]
    nb = BIAS_PAGES_PER_STEP
    n_steps = n_pages // nb

    def page_map(e):
        return lambda bi, s, pt: (pt[bi, s * nb + e], 0, 0)

    const2 = lambda bi, s, pt: (0, 0)
    in_specs = [pl.BlockSpec((1, 1, width), page_map(e)) for e in range(nb)]
    in_specs += [pl.BlockSpec(sfx.shape, const2), pl.BlockSpec(tot_m.shape, const2),
                 pl.BlockSpec(later_m.shape, const2), pl.BlockSpec(spread.shape, const2)]
    grid_spec = pltpu.PrefetchScalarGridSpec(
        num_scalar_prefetch=1,
        grid=(b, n_steps),
        in_specs=in_specs,
        out_specs=pl.BlockSpec((1, n_pages, width), lambda bi, s, pt: (bi, 0, 0)),
        scratch_shapes=[pltpu.VMEM((n_pages, width), F32)],
    )
    return pl.pallas_call(
        functools.partial(_bias_kernel, n_steps=n_steps),
        grid_spec=grid_spec,
        out_shape=jax.ShapeDtypeStruct((b, n_pages, width), F32),
        compiler_params=_params(2),
        name="decode_bias",
    )(page_table, *([lf_flat] * nb), sfx, tot_m, later_m, spread)


def _sattn_kernel(pt_ref, q_ref, ck_hbm, cv_hbm, bias_ref, mask_ref, kn_ref, vn_ref,
                  bn_ref, maskn_ref, o_ref, m_ref, l_ref, acc_ref, kbuf, vbuf, sem,
                  *, n_steps, n_grp):
    g_pages = PAGES_PER_STEP
    g = pl.program_id(1)
    step = pl.program_id(0) * n_steps + g
    n_total = pl.num_programs(0) * n_steps
    keys_g = mask_ref.shape[1]
    dh = q_ref.shape[3]

    def step_copies(t, slot):
        seq, grp = t // n_steps, t % n_steps
        out = []
        for e in range(g_pages):
            page = pt_ref[seq, grp * g_pages + e]
            out.append(pltpu.make_async_copy(ck_hbm.at[page], kbuf.at[slot, e], sem.at[slot, 0]))
            out.append(pltpu.make_async_copy(cv_hbm.at[page], vbuf.at[slot, e], sem.at[slot, 1]))
        return out

    @pl.when(step == 0)
    def _():
        for t in range(RING_DEPTH - 1):
            for c in step_copies(t, t):
                c.start()

    ahead = step + (RING_DEPTH - 1)

    @pl.when(ahead < n_total)
    def _():
        for c in step_copies(ahead, ahead % RING_DEPTH):
            c.start()

    slot = step % RING_DEPTH
    for c in step_copies(step, slot):
        c.wait()

    @pl.when(g == 0)
    def _():
        m_ref[...] = jnp.full(m_ref.shape, NEG_INF, F32)
        l_ref[...] = jnp.zeros(l_ref.shape, F32)
        acc_ref[...] = jnp.zeros(acc_ref.shape, F32)

    def update(gi, s, v_rows):
        m_old = m_ref[gi]
        m_new = jnp.maximum(m_old, jnp.max(s, axis=-1, keepdims=True))
        alpha = jnp.exp(m_old - m_new)
        p = jnp.exp(s - m_new)
        l_ref[gi] = alpha * l_ref[gi] + jnp.sum(p, axis=-1, keepdims=True)
        acc_ref[gi] = alpha * acc_ref[gi] + _dot(p.astype(BF16), v_rows)
        m_ref[gi] = m_new

    def group_rows(buf, e, gi):
        tiles = buf[slot, e, pl.ds(gi, PAGE_SIZE, stride=n_grp), :, :]
        return tiles.reshape(keys_g, dh).astype(BF16)

    steps_per_block = SUBLANES // g_pages
    bias_rows = [bias_ref[0, pl.ds((g % steps_per_block) * g_pages + e, 1), :]
                 for e in range(g_pages)]

    def group_scores(gi):
        parts = [_dot_nt(q_ref[0, gi], group_rows(kbuf, e, gi))
                 + (mask_ref[...] + bias_rows[e][:, gi * keys_g:(gi + 1) * keys_g])
                 for e in range(g_pages)]
        return jnp.concatenate(parts, axis=1)

    scores = [group_scores(gi) for gi in range(n_grp)]
    for gi in range(n_grp):
        values = jnp.concatenate([group_rows(vbuf, e, gi) for e in range(g_pages)], axis=0)
        update(gi, scores[gi], values)

    @pl.when(g == n_steps - 1)
    def _():
        for gi in range(n_grp):
            s = _dot_nt(q_ref[0, gi], kn_ref[0, gi]) + (maskn_ref[...] + bn_ref[0, gi])
            update(gi, s, vn_ref[0, gi])
        o_ref[0] = (acc_ref[...] / l_ref[...]).astype(o_ref.dtype)


def _sattn_call(page_table, q4, cache_k, cache_v, bias, mask, k_new, v_new,
                bias_new, mask_new):
    b, n_grp, rows, dh = q4.shape
    n_pages = page_table.shape[1]
    n_steps = n_pages // PAGES_PER_STEP
    page_tiles = cache_k.shape[1]
    width = bias.shape[2]
    steps_per_block = SUBLANES // PAGES_PER_STEP

    per_seq = lambda bi, g, pt: (bi, 0, 0, 0)
    const2 = lambda bi, g, pt: (0, 0)
    ring = (RING_DEPTH, PAGES_PER_STEP, page_tiles, SUBLANES, dh)
    in_specs = [
        pl.BlockSpec((1, n_grp, rows, dh), per_seq),
        pl.BlockSpec(memory_space=pl.ANY),
        pl.BlockSpec(memory_space=pl.ANY),
    ]
    in_specs += [
        pl.BlockSpec((1, SUBLANES, width), lambda bi, g, pt: (bi, g // steps_per_block, 0)),
        pl.BlockSpec(mask.shape, const2),
        pl.BlockSpec((1, n_grp, rows, dh), per_seq),
        pl.BlockSpec((1, n_grp, rows, dh), per_seq),
        pl.BlockSpec((1, n_grp, 1, rows), per_seq),
        pl.BlockSpec(mask_new.shape, const2),
    ]
    grid_spec = pltpu.PrefetchScalarGridSpec(
        num_scalar_prefetch=1,
        grid=(b, n_steps),
        in_specs=in_specs,
        out_specs=pl.BlockSpec((1, n_grp, rows, dh), per_seq),
        scratch_shapes=[pltpu.VMEM((n_grp, rows, 1), F32), pltpu.VMEM((n_grp, rows, 1), F32),
                        pltpu.VMEM((n_grp, rows, dh), F32),
                        pltpu.VMEM(ring, F32), pltpu.VMEM(ring, F32),
                        pltpu.SemaphoreType.DMA((RING_DEPTH, 2))],
    )
    kern = functools.partial(_sattn_kernel, n_steps=n_steps, n_grp=n_grp)
    return pl.pallas_call(
        kern,
        grid_spec=grid_spec,
        out_shape=jax.ShapeDtypeStruct((b, n_grp, rows, dh), BF16),
        compiler_params=_params(2),
        name="fox_decode",
    )(page_table, q4, cache_k, cache_v, bias, mask, k_new, v_new, bias_new, mask_new)


def _oproj_kernel(a_ref, w_ref, h_ref, o_ref):
    o_ref[...] = h_ref[...] + _dot(a_ref[...], w_ref[...])


def _oproj_call(a, w_o, h, tm, tn):
    m, d = h.shape
    return pl.pallas_call(
        _oproj_kernel,
        grid=(m // tm, d // tn),
        in_specs=[
            pl.BlockSpec((tm, d), lambda i, j: (i, 0)),
            pl.BlockSpec((d, tn), lambda i, j: (0, j)),
            pl.BlockSpec((tm, tn), lambda i, j: (i, j)),
        ],
        out_specs=pl.BlockSpec((tm, tn), lambda i, j: (i, j)),
        out_shape=jax.ShapeDtypeStruct((m, d), F32),
        compiler_params=_params(2),
        name="o_proj",
    )(a, w_o, h)


def _cumsum_matrix(n, group):
    t = np.arange(n)[:, None]
    tp = np.arange(n)[None, :]
    return ((tp <= t) & (tp // group == t // group)).astype(BF16)


def _placement(nh, d):
    lane = np.arange(LANES)[None, :, None]
    col = np.arange(d)[None, None, :]
    piece = np.arange(N_SPLIT)[:, None, None]
    return ((lane < nh) & (col == lane * HEAD_DIM + piece)).astype(BF16)


def kernel(x_prompt, x_sample, state_conv_a, state_ffn, cache_k, cache_v, cache_logf, page_table, p_prompt, p_sample, g_mix_a, w_pw1, w_dwa, b_dwa, ln_g, ln_b, w_pw2, g_kv, w_k, w_v, g_k, w_f, b_f, g_mix_b, w_q, g_q, w_o, g_ffn, w_up, w_gate, w_fdw, b_fdw, w_down, g_ple, w_ple_gate, w_ple_proj):
    n_p, t_p, d = x_prompt.shape
    n_s, t_s, _ = x_sample.shape
    depth = g_ffn.shape[0]
    n_a = g_mix_a.shape[0]
    f = w_up.shape[2]
    nh = d // HEAD_DIM
    m_s = n_s * t_s

    bf = lambda w: w.astype(BF16)
    layers = lambda w: [bf(w[i]) for i in range(w.shape[0])]
    w_pw1_b, w_pw2_b = layers(w_pw1), layers(w_pw2)
    w_k_b, w_v_b, w_q_b, w_o_b = bf(w_k), bf(w_v), layers(w_q), layers(w_o)
    w_up_b, w_gate_b, w_down_b = bf(w_up), bf(w_gate), bf(w_down)
    w_pg_b, w_pp_b = bf(w_ple_gate), bf(w_ple_proj)
    w_f_pad = jnp.pad(bf(w_f), ((0, 0), (0, LANES - nh)))
    b_f_pad = jnp.pad(b_f, (0, LANES - nh)).reshape(1, LANES)
    place = _placement(nh, d)
    row = lambda a: a.reshape(1, -1)

    def trunk(x, p, n_seq, seq_len, a_prev32, tiles, ffn, attend, tri, transposed):
        h = x.reshape(n_seq * seq_len, d)
        a_us, f_bufs = [], []
        k = v = lf = None
        for i in range(depth):
            attn = None
            if i < n_a:
                u = _glu_call(h, row(g_mix_a[i]), w_pw1_b[i], tiles.rows, tiles.cols)
                h = _conva_call(u, a_prev32[i], w_dwa[i], row(b_dwa[i]), row(ln_g[i]),
                                row(ln_b[i]), w_pw2_b[i], h, tiles.rows_conv, seq_len)
                a_us.append(u.reshape(n_seq, seq_len, d))
            else:
                jb = i - n_a
                outs = _qkv_call(
                    h, row(g_kv), row(g_mix_b[jb]), w_k_b, w_v_b, w_q_b[jb],
                    row(g_k), row(g_q[jb]), w_f_pad, b_f_pad, tri, place,
                    tiles.rows_qkv, tiles.cols, seq_len, transposed)
                k, v = outs[0], outs[1]
                lf = outs[6] if transposed else outs[5]
                attn = (attend(outs), w_o_b[jb])
            h, fb = ffn(i, h, attn)
            f_bufs.append(fb)
            h = _ple_call(h, row(g_ple[i]), p[i].reshape(n_seq * seq_len, -1),
                          w_pg_b, w_pp_b, tiles.rows_ple, i)
        shp = (n_seq, seq_len, nh, HEAD_DIM)
        return (h.reshape(n_seq, seq_len, d), a_us, jnp.stack(f_bufs),
                k.reshape(shp), v.reshape(shp),
                lf[:, :nh].reshape(n_seq, seq_len, nh))

    tp = PROMPT_TILES

    def ffn_prompt(i, h, attn):
        prev = jnp.zeros((n_p, CONV_F_WIDTH - 1, f), F32)
        return _ffn_seq_call(h, row(g_ffn[i]), w_up_b, w_gate_b, w_fdw[i],
                             row(b_fdw[i]), w_down_b, prev, tp.rows_ffn, tp.cols, t_p, i,
                             attn)

    def attend_prompt(outs):
        _, _, kb, kx, v_t, q_t, _ = outs
        return _attn_call(q_t, kb, kx, v_t, n_p, t_p, tp.rows_qkv)

    a_prev_p = [jnp.zeros((n_p, HALO_ROWS, d), F32) for _ in range(n_a)]
    (y_p, a_us_p, ffn_p, k_p, v_p, lf_p) = trunk(
        x_prompt, p_prompt, n_p, t_p, a_prev_p, tp, ffn_prompt,
        attend_prompt, _cumsum_matrix(tp.rows_qkv, tp.rows_qkv), True)
    conv_a_p = jnp.stack([u[:, t_p - (CONV_A_WIDTH - 1):] for u in a_us_p])

    def ffn_sample(i, h, attn):
        if attn is not None:
            h = _oproj_call(attn[0], attn[1], h, m_s, tp.cols)
        st = state_ffn[i]
        z = jnp.zeros((n_s, t_s, f), F32)
        p1 = z.at[:, 0].set(st[:, 1]).reshape(m_s, f)
        p2 = z.at[:, 0].set(st[:, 0]).at[:, 1].set(st[:, 1]).reshape(m_s, f)
        h, u = _ffn_multi_call(h, row(g_ffn[i]), w_up_b, w_gate_b, w_fdw[i],
                               row(b_fdw[i]), w_down_b, p1, p2, tp.cols, t_s, i)
        full = jnp.concatenate([st, u.reshape(n_s, t_s, f)], axis=1)
        return h, full[:, -(CONV_F_WIDTH - 1):]

    n_pool = cache_k.shape[0]
    n_pages = page_table.shape[1]
    n_grp = nh // SUBLANES
    keys_g = PAGE_SIZE * SUBLANES
    page_rows = PAGE_SIZE * nh
    tiles_shape = (n_pool, PAGE_SIZE * n_grp, SUBLANES, HEAD_DIM)
    cache_k4 = cache_k.reshape(tiles_shape)
    cache_v4 = cache_v.reshape(tiles_shape)
    lf_flat = cache_logf.reshape(n_pool, 1, page_rows)
    fr = np.arange(page_rows)
    t_in, h_in = fr // nh, fr % nh
    t_out = (fr % keys_g) // SUBLANES
    h_out = (fr // keys_g) * SUBLANES + fr % SUBLANES
    lane = np.arange(LANES)
    sfx = ((h_in[:, None] == h_out[None, :]) & (t_in[:, None] > t_out[None, :])).astype(BF16)
    tot_m = (h_in[:, None] == lane[None, :]).astype(BF16)
    pg = np.arange(n_pages)
    later_m = (pg[None, :] > pg[:, None]).astype(BF16)
    spread = (lane[:, None] == h_out[None, :]).astype(BF16)
    qrow = np.arange(t_s * SUBLANES)
    kcol = np.arange(keys_g)
    mask = np.where((qrow[:, None] % SUBLANES) == (kcol[None, :] % SUBLANES),
                    0.0, NEG_INF).astype(np.float32)
    mask_new = np.where(((qrow[:, None] % SUBLANES) == (qrow[None, :] % SUBLANES))
                        & (qrow[None, :] // SUBLANES <= qrow[:, None] // SUBLANES),
                        0.0, NEG_INF).astype(np.float32)
    bias = _bias_call(page_table, lf_flat, sfx, tot_m, later_m, spread)

    def by_group(a, width):
        a5 = a.reshape(n_s, t_s, n_grp, SUBLANES, width)
        return jnp.transpose(a5, (0, 2, 1, 3, 4)).reshape(n_s, n_grp, t_s * SUBLANES, width)

    def attend_sample(outs):
        _, _, kb, vb, q, _, c = outs
        bias_new = -by_group(c[:, :nh], 1).reshape(n_s, n_grp, 1, t_s * SUBLANES)
        a = _sattn_call(page_table, by_group(q, HEAD_DIM), cache_k4, cache_v4, bias, mask,
                        by_group(kb, HEAD_DIM), by_group(vb, HEAD_DIM), bias_new, mask_new)
        a5 = a.reshape(n_s, n_grp, t_s, SUBLANES, HEAD_DIM)
        return jnp.transpose(a5, (0, 2, 1, 3, 4)).reshape(m_s, d)

    a_prev_s = [jnp.pad(state_conv_a[i], ((0, 0), (HALO_ROWS - (CONV_A_WIDTH - 1), 0), (0, 0)))
                for i in range(n_a)]
    ts = _Tiles(rows=m_s, rows_qkv=m_s, rows_ffn=m_s, rows_conv=t_s, rows_ple=m_s,
                cols=tp.cols)
    (y_s, a_us_s, ffn_s, k_s, v_s, lf_s) = trunk(
        x_sample, p_sample, n_s, t_s, a_prev_s, ts, ffn_sample,
        attend_sample, _cumsum_matrix(m_s, t_s), False)
    conv_a_s = jnp.stack([
        jnp.concatenate([state_conv_a[i], a_us_s[i]], axis=1)[:, -(CONV_A_WIDTH - 1):]
        for i in range(n_a)])

    return (y_p, y_s, conv_a_p, ffn_p, k_p, v_p, lf_p,
            conv_a_s, ffn_s, k_s, v_s, lf_s)
```
